```python
import math
import jax, jax.numpy as jnp
from jax import lax
import numpy as np

D_MODEL = 4096
BATCH = 2
SEQ = 4096
DEPTH = 2

CHUNK = 64
N_MIXERS = 2
EXPAND = 2
D_INNER = EXPAND * D_MODEL
HEAD_DIM = 128
N_HEADS = D_INNER // HEAD_DIM
HGRN_EXPAND = 128
HGRN_KEY = N_HEADS * HGRN_EXPAND
HGRN_BLOCK = CHUNK // 4
Q_BLOCK = 2 * CHUNK
N_A = (DEPTH + 1) // 2
N_B = DEPTH // 2
EPS = 1e-6

kernel_name = "hgrn2_fox_interleaved_hybrid"


def rms_norm(x, w):
    xf = x.astype(jnp.float32)
    y = xf * lax.rsqrt(jnp.mean(xf * xf, axis=-1, keepdims=True) + EPS)
    return (y * w.astype(jnp.float32)).astype(x.dtype)


def hgrn2_chunkwise(q, k, v, log_f):
    B, S, H, DK = q.shape
    DV = v.shape[-1]
    L = HGRN_BLOCK
    N = S // L

    def to_blocks(t):
        return t.reshape(B, N, L, H, t.shape[-1]).transpose(1, 0, 3, 2, 4)

    qb, kb, vb, gb = (to_blocks(t) for t in (q, k, v, log_f))
    mask = jnp.tril(jnp.ones((L, L), dtype=bool))

    def step(state, xs):
        qc, kc, vc, gc = xs
        G = jnp.cumsum(gc, axis=-2)
        q_dec = qc * jnp.exp(G)
        k_inv = kc * jnp.exp(-G)
        scores = jnp.where(mask, jnp.einsum('bhtd,bhsd->bhts', q_dec, k_inv), 0.0)
        o = (jnp.einsum('bhts,bhsv->bhtv', scores, vc)
             + jnp.einsum('bhtd,bhdv->bhtv', q_dec, state))
        G_last = G[:, :, -1:, :]
        k_end = kc * jnp.exp(G_last - G)
        state = (jnp.exp(G_last[:, :, 0, :])[..., None] * state
                 + jnp.einsum('bhsd,bhsv->bhdv', k_end, vc))
        return state, o

    state0 = jnp.zeros((B, H, DK, DV), jnp.float32)
    _, o = lax.scan(step, state0, (qb, kb, vb, gb))
    return o.transpose(1, 0, 3, 2, 4).reshape(B, S, H, DV)


def hgrn2_mixer(h, w_in, o_norm, lb):
    B, S, _ = h.shape
    proj = jnp.einsum('bsd,de->bse', h, w_in)
    q, f, i, gate = jnp.split(proj, 4, axis=-1)
    forget = lb + (1.0 - lb) * jax.nn.sigmoid(f.astype(jnp.float32))
    k = 1.0 - forget
    log_f = jnp.log(forget)
    qh = jax.nn.silu(q.astype(jnp.float32)).reshape(B, S, N_HEADS, HGRN_EXPAND)
    kh = k.reshape(B, S, N_HEADS, HGRN_EXPAND)
    gh = log_f.reshape(B, S, N_HEADS, HGRN_EXPAND)
    vh = i.astype(jnp.float32).reshape(B, S, N_HEADS, HEAD_DIM)
    o = hgrn2_chunkwise(qh, kh, vh, gh)
    o = o * lax.rsqrt(jnp.mean(o * o, axis=-1, keepdims=True) + EPS)
    o = o.reshape(B, S, D_INNER) * o_norm.astype(jnp.float32)
    return (o * jax.nn.silu(gate.astype(jnp.float32))).astype(h.dtype)


def fox_mixer(h, w_in, b_f):
    B, S, _ = h.shape
    proj = jnp.einsum('bsd,de->bse', h, w_in)
    q, k, v, gate = jnp.split(proj[..., :4 * D_INNER], 4, axis=-1)
    f_logit = proj[..., 4 * D_INNER:].astype(jnp.float32) + b_f.astype(jnp.float32)
    c = jnp.cumsum(jax.nn.log_sigmoid(f_logit), axis=1).transpose(0, 2, 1)
    qh = q.reshape(B, S, N_HEADS, HEAD_DIM)
    kh = k.reshape(B, S, N_HEADS, HEAD_DIM)
    vh = v.reshape(B, S, N_HEADS, HEAD_DIM)
    scale = 1.0 / math.sqrt(HEAD_DIM)
    outs = []
    for start in range(0, S, Q_BLOCK):
        end = start + Q_BLOCK
        logits = jnp.einsum('bqhd,bkhd->bhqk', qh[:, start:end], kh[:, :end]).astype(jnp.float32) * scale
        logits = logits + c[:, :, start:end, None] - c[:, :, None, :end]
        q_pos = start + jnp.arange(Q_BLOCK)[:, None]
        k_pos = jnp.arange(end)[None, :]
        logits = jnp.where(q_pos >= k_pos, logits, -jnp.inf)
        p = jax.nn.softmax(logits, axis=-1)
        outs.append(jnp.einsum('bhqk,bkhd->bqhd', p.astype(vh.dtype), vh[:, :end]))
    o = jnp.concatenate(outs, axis=1).reshape(B, S, D_INNER)
    return (o.astype(jnp.float32) * jax.nn.silu(gate.astype(jnp.float32))).astype(h.dtype)


def setup_inputs(seed: int = 0) -> dict:
    key = jax.random.key(seed)
    ks = jax.random.split(key, 12)
    f32 = jnp.float32
    x = jax.random.normal(ks[0], (BATCH, SEQ, D_MODEL), f32)
    norm_w = 1.0 + 0.02 * jax.random.normal(ks[1], (DEPTH, D_MODEL), f32)
    w_in_a = jax.random.normal(ks[2], (N_A, D_MODEL, 4 * D_INNER), f32) * D_MODEL ** -0.5
    lb_logits = 0.5 * jax.random.normal(ks[3], (N_A + 1, HGRN_KEY), f32)
    o_norm_a = 1.0 + 0.02 * jax.random.normal(ks[4], (N_A, D_INNER), f32)
    w_out_a = jax.random.normal(ks[5], (N_A, D_INNER, D_MODEL), f32) * D_INNER ** -0.5
    w_in_b = jax.random.normal(ks[6], (N_B, D_MODEL, 4 * D_INNER + N_HEADS), f32) * D_MODEL ** -0.5
    b_f = jax.random.uniform(ks[7], (N_B, N_HEADS), f32, 1.0, 4.0)
    w_out_b = jax.random.normal(ks[8], (N_B, D_INNER, D_MODEL), f32) * D_INNER ** -0.5
    final_norm = 1.0 + 0.02 * jax.random.normal(ks[9], (D_MODEL,), f32)
    return {"x": x, "norm_w": norm_w, "w_in_a": w_in_a, "lb_logits": lb_logits,
            "o_norm_a": o_norm_a, "w_out_a": w_out_a, "w_in_b": w_in_b, "b_f": b_f,
            "w_out_b": w_out_b, "final_norm": final_norm}


def reference(x, norm_w, w_in_a, lb_logits, o_norm_a, w_out_a, w_in_b, b_f, w_out_b, final_norm):
    lower_bounds = jnp.cumsum(jax.nn.softmax(lb_logits.astype(jnp.float32), axis=0), axis=0)
    for i in range(DEPTH):
        h = rms_norm(x, norm_w[i])
        j = i // N_MIXERS
        if i % N_MIXERS == 0:
            y = hgrn2_mixer(h, w_in_a[j], o_norm_a[j], lower_bounds[j])
            x = x + jnp.einsum('bse,ed->bsd', y, w_out_a[j])
        else:
            y = fox_mixer(h, w_in_b[j], b_f[j])
            x = x + jnp.einsum('bse,ed->bsd', y, w_out_b[j])
    return rms_norm(x, final_norm)
```

```python
import functools
import math

import jax
import jax.numpy as jnp
from jax import lax
from jax.experimental import pallas as pl
from jax.experimental.pallas import tpu as pltpu

F32 = jnp.float32
BF16 = jnp.bfloat16

EPS = 1e-6
HEAD_DIM = 128
HGRN_BLOCK = 16
VMEM_LIMIT = 56 * 1024 * 1024

_NT = (((1,), (1,)), ((), ()))
_TN = (((0,), (0,)), ((), ()))


def _params(*sem):
    return pltpu.CompilerParams(dimension_semantics=sem, vmem_limit_bytes=VMEM_LIMIT)


def _rmsnorm_kernel(x_ref, w_ref, o_ref):
    x = x_ref[...]
    y = x * lax.rsqrt(jnp.mean(x * x, axis=-1, keepdims=True) + EPS)
    o_ref[...] = (y * w_ref[...]).astype(o_ref.dtype)


def _rmsnorm(x, w, out_dtype, bm=256):
    m, d = x.shape
    return pl.pallas_call(
        _rmsnorm_kernel,
        grid=(m // bm,),
        in_specs=[pl.BlockSpec((bm, d), lambda i: (i, 0)),
                  pl.BlockSpec((1, d), lambda i: (0, 0))],
        out_specs=pl.BlockSpec((bm, d), lambda i: (i, 0)),
        out_shape=jax.ShapeDtypeStruct((m, d), out_dtype),
        compiler_params=_params("arbitrary"),
        name="rmsnorm",
    )(x, w.reshape(1, d))


def _mm_kernel(a_ref, b_ref, o_ref):
    o_ref[...] = jnp.dot(a_ref[...], b_ref[...], preferred_element_type=F32).astype(o_ref.dtype)


def _in_proj(a, b, n, bm=1024, bn=1024):
    m, k = a.shape
    return pl.pallas_call(
        _mm_kernel,
        grid=(n // bn, m // bm),
        in_specs=[pl.BlockSpec((bm, k), lambda j, i: (i, 0)),
                  pl.BlockSpec((k, bn), lambda j, i: (0, j))],
        out_specs=pl.BlockSpec((bm, bn), lambda j, i: (i, j)),
        out_shape=jax.ShapeDtypeStruct((m, n), BF16),
        compiler_params=_params("arbitrary", "arbitrary"),
        name="in_proj",
    )(a, b)


def _mm_res_kernel(a_ref, b_ref, r_ref, o_ref):
    d = jnp.dot(a_ref[...], b_ref[...], preferred_element_type=F32)

    @pl.when(pl.program_id(2) == 0)
    def _():
        o_ref[...] = r_ref[...] + d

    @pl.when(pl.program_id(2) != 0)
    def _():
        o_ref[...] += d


def _out_proj(a, b, res, bm=1024, bn=1024, bk=2048):
    m, k = a.shape
    n = b.shape[1]
    return pl.pallas_call(
        _mm_res_kernel,
        grid=(n // bn, m // bm, k // bk),
        in_specs=[pl.BlockSpec((bm, bk), lambda j, i, kk: (i, kk)),
                  pl.BlockSpec((bk, bn), lambda j, i, kk: (kk, j)),
                  pl.BlockSpec((bm, bn), lambda j, i, kk: (i, j))],
        out_specs=pl.BlockSpec((bm, bn), lambda j, i, kk: (i, j)),
        out_shape=jax.ShapeDtypeStruct((m, n), F32),
        compiler_params=_params("arbitrary", "arbitrary", "arbitrary"),
        name="out_proj",
    )(a, b, res)


def _silu(x):
    return x * jax.nn.sigmoid(x)


def _hgrn_kernel(q_ref, f_ref, i_ref, g_ref, lbl_ref, on_ref, y_ref, st_ref, *, layer, tile, chunk):
    @pl.when(pl.program_id(2) == 0)
    def _():
        st_ref[...] = jnp.zeros_like(st_ref)

    lbl = lbl_ref[...]
    e = jnp.exp(lbl - jnp.max(lbl, axis=0, keepdims=True))
    lb = jnp.sum(e[:layer + 1], axis=0, keepdims=True) / jnp.sum(e, axis=0, keepdims=True)
    o_gain = on_ref[...]

    r = lax.broadcasted_iota(jnp.int32, (chunk, chunk), 0)
    s = lax.broadcasted_iota(jnp.int32, (chunk, chunk), 1)
    causal = s <= r
    tri = causal.astype(F32)
    a_mid = tri - (s < chunk // 2).astype(F32)
    a_end = (s > r).astype(F32)
    a_all = jnp.concatenate([tri, a_mid, a_end], axis=0)

    def body(c, carry):
        rows = pl.ds(pl.multiple_of(c * chunk, chunk), chunk)
        q = q_ref[rows, :].astype(F32)
        f = f_ref[rows, :].astype(F32)
        v = i_ref[rows, :]
        gate = g_ref[rows, :].astype(F32)

        forget = lb + (1.0 - lb) * jax.nn.sigmoid(f)
        k = 1.0 - forget
        log_f = jnp.log(forget)
        qs = _silu(q)

        gg = jnp.dot(a_all, log_f, precision=lax.Precision.HIGHEST, preferred_element_type=F32)
        g_inc = gg[:chunk]
        g_mid = gg[chunk:2 * chunk]
        g_end = gg[2 * chunk:]

        q_dec = (qs * jnp.exp(g_inc)).astype(BF16)
        q_mid = (qs * jnp.exp(g_mid)).astype(BF16)
        k_mid = (k * jnp.exp(-g_mid)).astype(BF16)
        k_end = (k * jnp.exp(g_end)).astype(BF16)

        scores = lax.dot_general(q_mid, k_mid, _NT, preferred_element_type=F32)
        scores = jnp.where(causal, scores, 0.0).astype(BF16)
        st = st_ref[...]
        o = (jnp.dot(scores, v, preferred_element_type=F32)
             + lax.dot_general(q_dec, st.astype(BF16), _NT, preferred_element_type=F32))
        st_ref[...] = (st * jnp.exp(g_inc[chunk - 1:chunk, :])
                       + lax.dot_general(v, k_end, _TN, preferred_element_type=F32))

        o = o * lax.rsqrt(jnp.mean(o * o, axis=-1, keepdims=True) + EPS)
        y_ref[rows, :] = (o * o_gain * _silu(gate)).astype(y_ref.dtype)
        return carry

    lax.fori_loop(0, tile // chunk, body, 0)


def _hgrn_mixer(proj, lb_logits, o_norm, *, layer, batch, seq, tile=512):
    m, four_inner = proj.shape
    d_inner = four_inner // 4
    heads = d_inner // HEAD_DIM
    nt = seq // tile
    chunk = 2 * HGRN_BLOCK

    def stream(idx):
        return pl.BlockSpec((tile, HEAD_DIM), lambda b, h, t: (b * nt + t, idx * heads + h))

    rows = lb_logits.shape[0]
    return pl.pallas_call(
        functools.partial(_hgrn_kernel, layer=layer, tile=tile, chunk=chunk),
        grid=(batch, heads, nt),
        in_specs=[stream(0), stream(1), stream(2), stream(3),
                  pl.BlockSpec((rows, HEAD_DIM), lambda b, h, t: (0, h)),
                  pl.BlockSpec((1, HEAD_DIM), lambda b, h, t: (0, h))],
        out_specs=pl.BlockSpec((tile, HEAD_DIM), lambda b, h, t: (b * nt + t, h)),
        out_shape=jax.ShapeDtypeStruct((m, d_inner), BF16),
        scratch_shapes=[pltpu.VMEM((HEAD_DIM, HEAD_DIM), F32)],
        compiler_params=_params("arbitrary", "arbitrary", "arbitrary"),
        name="hgrn2_mixer",
    )(proj, proj, proj, proj, lb_logits, o_norm.reshape(1, d_inner))


def _fgate_kernel(wt_ref, h_ref, b_ref, c_ref, carry_ref, *, bs):
    @pl.when(pl.program_id(1) == 0)
    def _():
        carry_ref[...] = jnp.zeros_like(carry_ref)

    z = lax.dot_general(wt_ref[...], h_ref[...], _NT, preferred_element_type=F32) + b_ref[...]
    ls = jnp.minimum(z, 0.0) - jnp.log1p(jnp.exp(-jnp.abs(z)))
    r = lax.broadcasted_iota(jnp.int32, (bs, bs), 0)
    s = lax.broadcasted_iota(jnp.int32, (bs, bs), 1)
    upper = (r <= s).astype(F32)
    cs = jnp.dot(ls, upper, precision=lax.Precision.HIGHEST, preferred_element_type=F32) + carry_ref[...]
    c_ref[0] = cs
    carry_ref[...] = cs[:, bs - 1:bs]


def _forget_cumsum(h, w_f_t, b_f, *, batch, seq, bs=512):
    heads, d = w_f_t.shape
    nb = seq // bs
    return pl.pallas_call(
        functools.partial(_fgate_kernel, bs=bs),
        grid=(batch, nb),
        in_specs=[pl.BlockSpec((heads, d), lambda b, t: (0, 0)),
                  pl.BlockSpec((bs, d), lambda b, t: (b * nb + t, 0)),
                  pl.BlockSpec((heads, 1), lambda b, t: (0, 0))],
        out_specs=pl.BlockSpec((1, heads, bs), lambda b, t: (b, 0, t)),
        out_shape=jax.ShapeDtypeStruct((batch, heads, seq), F32),
        scratch_shapes=[pltpu.VMEM((heads, 1), F32)],
        compiler_params=_params("arbitrary", "arbitrary"),
        name="forget_cumsum",
    )(w_f_t, h, b_f.reshape(heads, 1))


def _fox_kernel(q_ref, k_ref, v_ref, g_ref, c_ref, y_ref, m_ref, l_ref, acc_ref, *, blk, scale):
    i = pl.program_id(2)
    q = q_ref[...]
    m_ref[...] = jnp.full_like(m_ref, -jnp.inf)
    l_ref[...] = jnp.zeros_like(l_ref)
    acc_ref[...] = jnp.zeros_like(acc_ref)

    def step(j, masked):
        cols = pl.ds(pl.multiple_of(j * blk, blk), blk)
        k = k_ref[cols, :]
        v = v_ref[cols, :]
        sc = lax.dot_general(q, k, _NT, preferred_element_type=F32) * scale - c_ref[0, pl.ds(j, 1), :]
        if masked:
            r = lax.broadcasted_iota(jnp.int32, (blk, blk), 0)
            s = lax.broadcasted_iota(jnp.int32, (blk, blk), 1)
            sc = jnp.where(s <= r, sc, -jnp.inf)
        m_prev = m_ref[...]
        m_new = jnp.maximum(m_prev, jnp.max(sc, axis=-1, keepdims=True))
        alpha = jnp.exp(m_prev - m_new)
        p = jnp.exp(sc - m_new)
        l_ref[...] = alpha * l_ref[...] + jnp.sum(p, axis=-1, keepdims=True)
        acc_ref[...] = alpha * acc_ref[...] + jnp.dot(p.astype(BF16), v, preferred_element_type=F32)
        m_ref[...] = m_new

    def body(j, carry):
        step(j, False)
        return carry

    lax.fori_loop(0, i, body, 0)
    step(i, True)

    o = acc_ref[...] / l_ref[...]
    y_ref[...] = (o * _silu(g_ref[...].astype(F32))).astype(y_ref.dtype)


def _fox_attention(proj, c, *, batch, seq, blk=256):
    m = proj.shape[0]
    d_inner = proj.shape[1] // 4
    heads = d_inner // HEAD_DIM
    nq = seq // blk
    c_blocks = c.reshape(batch * heads, nq, blk)
    scale = 1.0 / math.sqrt(HEAD_DIM)

    return pl.pallas_call(
        functools.partial(_fox_kernel, blk=blk, scale=scale),
        grid=(batch, heads, nq),
        in_specs=[pl.BlockSpec((blk, HEAD_DIM), lambda b, h, i: (b * nq + i, h)),
                  pl.BlockSpec((seq, HEAD_DIM), lambda b, h, i: (b, heads + h)),
                  pl.BlockSpec((seq, HEAD_DIM), lambda b, h, i: (b, 2 * heads + h)),
                  pl.BlockSpec((blk, HEAD_DIM), lambda b, h, i: (b * nq + i, 3 * heads + h)),
                  pl.BlockSpec((1, nq, blk), lambda b, h, i: (b * heads + h, 0, 0))],
        out_specs=pl.BlockSpec((blk, HEAD_DIM), lambda b, h, i: (b * nq + i, h)),
        out_shape=jax.ShapeDtypeStruct((m, d_inner), BF16),
        scratch_shapes=[pltpu.VMEM((blk, 1), F32),
                        pltpu.VMEM((blk, 1), F32),
                        pltpu.VMEM((blk, HEAD_DIM), F32)],
        compiler_params=_params("arbitrary", "arbitrary", "arbitrary"),
        name="fox_attention",
    )(proj, proj, proj, proj, c_blocks)


def kernel(x, norm_w, w_in_a, lb_logits, o_norm_a, w_out_a, w_in_b, b_f, w_out_b, final_norm):
    batch, seq, d_model = x.shape
    depth = norm_w.shape[0]
    xr = x.reshape(batch * seq, d_model)
    for layer in range(depth):
        h = _rmsnorm(xr, norm_w[layer], BF16)
        j = layer // 2
        if layer % 2 == 0:
            d_inner = w_out_a.shape[1]
            proj = _in_proj(h, w_in_a[j].astype(BF16), 4 * d_inner)
            y = _hgrn_mixer(proj, lb_logits, o_norm_a[j], layer=j, batch=batch, seq=seq)
            xr = _out_proj(y, w_out_a[j].astype(BF16), xr)
        else:
            d_inner = w_out_b.shape[1]
            w_b = w_in_b[j].astype(BF16)
            proj = _in_proj(h, w_b, 4 * d_inner)
            c = _forget_cumsum(h, w_b[:, 4 * d_inner:].T, b_f[j], batch=batch, seq=seq)
            y = _fox_attention(proj, c, batch=batch, seq=seq)
            xr = _out_proj(y, w_out_b[j].astype(BF16), xr)
    return _rmsnorm(xr, final_norm, x.dtype).reshape(batch, seq, d_model)
```

```python
import functools
import math

import jax
import jax.numpy as jnp
from jax import lax
from jax.experimental import pallas as pl
from jax.experimental.pallas import tpu as pltpu

F32 = jnp.float32
BF16 = jnp.bfloat16

EPS = 1e-6
HEAD_DIM = 128
HGRN_BLOCK = 16
HGRN_CHUNK = 8 * HGRN_BLOCK
VMEM_LIMIT = 56 * 1024 * 1024
LOG2E = math.log2(math.e)

_NT = (((1,), (1,)), ((), ()))
_TN = (((0,), (0,)), ((), ()))


def _params(*sem):
    return pltpu.CompilerParams(dimension_semantics=sem, vmem_limit_bytes=VMEM_LIMIT)


def _rmsnorm_kernel(x_ref, w_ref, o_ref):
    x = x_ref[...]
    y = x * lax.rsqrt(jnp.mean(x * x, axis=-1, keepdims=True) + EPS)
    o_ref[...] = (y * w_ref[...]).astype(o_ref.dtype)


def _rmsnorm(x, w, out_dtype, bm=256):
    m, d = x.shape
    return pl.pallas_call(
        _rmsnorm_kernel,
        grid=(m // bm,),
        in_specs=[pl.BlockSpec((bm, d), lambda i: (i, 0)),
                  pl.BlockSpec((1, d), lambda i: (0, 0))],
        out_specs=pl.BlockSpec((bm, d), lambda i: (i, 0)),
        out_shape=jax.ShapeDtypeStruct((m, d), out_dtype),
        compiler_params=_params("arbitrary"),
        name="rmsnorm",
    )(x, w.reshape(1, d))


def _mm_kernel(a_ref, b_ref, o_ref, *, scaled_blocks, scale):
    acc = jnp.dot(a_ref[...], b_ref[...], preferred_element_type=F32)
    if scaled_blocks:
        acc = acc * jnp.where(pl.program_id(0) < scaled_blocks, scale, 1.0)
    o_ref[...] = acc.astype(o_ref.dtype)


def _in_proj(a, b, n, scaled_cols=0, scale=1.0, bm=1024, bn=1024):
    m, k = a.shape
    return pl.pallas_call(
        functools.partial(_mm_kernel, scaled_blocks=scaled_cols // bn, scale=scale),
        grid=(n // bn, m // bm),
        in_specs=[pl.BlockSpec((bm, k), lambda j, i: (i, 0)),
                  pl.BlockSpec((k, bn), lambda j, i: (0, j))],
        out_specs=pl.BlockSpec((bm, bn), lambda j, i: (i, j)),
        out_shape=jax.ShapeDtypeStruct((m, n), BF16),
        compiler_params=_params("arbitrary", "arbitrary"),
        name="in_proj",
    )(a, b)


def _mm_res_kernel(a_ref, b_ref, r_ref, o_ref):
    d = jnp.dot(a_ref[...], b_ref[...], preferred_element_type=F32)

    @pl.when(pl.program_id(2) == 0)
    def _():
        o_ref[...] = r_ref[...] + d

    @pl.when(pl.program_id(2) != 0)
    def _():
        o_ref[...] += d


def _out_proj(a, b, res, bm=1024, bn=1024, bk=2048):
    m, k = a.shape
    n = b.shape[1]
    return pl.pallas_call(
        _mm_res_kernel,
        grid=(n // bn, m // bm, k // bk),
        in_specs=[pl.BlockSpec((bm, bk), lambda j, i, kk: (i, kk)),
                  pl.BlockSpec((bk, bn), lambda j, i, kk: (kk, j)),
                  pl.BlockSpec((bm, bn), lambda j, i, kk: (i, j))],
        out_specs=pl.BlockSpec((bm, bn), lambda j, i, kk: (i, j)),
        out_shape=jax.ShapeDtypeStruct((m, n), F32),
        compiler_params=_params("arbitrary", "arbitrary", "arbitrary"),
        name="out_proj",
    )(a, b, res)


def _silu(x):
    return x * jax.nn.sigmoid(x)


def _hgrn_kernel(q_ref, f_ref, i_ref, g_ref, lbl_ref, on_ref, y_ref, st_ref, *, layer, tile, group):
    c = HGRN_CHUNK
    h2, h4, h8 = c // 2, c // 4, c // 8

    @pl.when(pl.program_id(2) == 0)
    def _():
        st_ref[...] = jnp.zeros_like(st_ref)

    lbl = lbl_ref[...]
    e = jnp.exp(lbl - jnp.max(lbl, axis=0, keepdims=True))
    lb_all = jnp.sum(e[:layer + 1], axis=0, keepdims=True) / jnp.sum(e, axis=0, keepdims=True)
    o_gain_all = on_ref[...]

    r = lax.broadcasted_iota(jnp.int32, (c, c), 0)
    s = lax.broadcasted_iota(jnp.int32, (c, c), 1)
    tri = (s <= r).astype(BF16)
    tri3 = jnp.concatenate([tri, tri, tri], axis=1)
    same_quarter = (r // h4) == (s // h4)
    level3_mask = jnp.logical_and(same_quarter, s <= r)

    def bcast(row, n):
        return jnp.broadcast_to(row, (n, HEAD_DIM))

    def one_head(rows, g):
        cols = slice(g * HEAD_DIM, (g + 1) * HEAD_DIM)
        lb = lb_all[:, cols]
        q = q_ref[rows, cols].astype(F32)
        f = f_ref[rows, cols].astype(F32)
        v = i_ref[rows, cols]
        gate = g_ref[rows, cols].astype(F32)

        forget = lb + (1.0 - lb) * jax.nn.sigmoid(f)
        k = 1.0 - forget
        log_f = jnp.log(forget)
        qs = _silu(q)

        hi = log_f.astype(BF16)
        rem = log_f - hi.astype(F32)
        mid = rem.astype(BF16)
        lo = (rem - mid.astype(F32)).astype(BF16)
        gc = jnp.dot(tri3, jnp.concatenate([hi, mid, lo], axis=0), preferred_element_type=F32)

        zeros2 = jnp.zeros((h2, HEAD_DIM), BF16)
        zeros4 = jnp.zeros((h4, HEAD_DIM), BF16)

        ref1 = gc[h2 - 1:h2, :]
        q1 = (qs[h2:] * jnp.exp(gc[h2:] - ref1)).astype(BF16)
        k1 = (k[:h2] * jnp.exp(ref1 - gc[:h2])).astype(BF16)
        s1 = lax.dot_general(jnp.concatenate([zeros2, q1], axis=0),
                             jnp.concatenate([k1, zeros2], axis=0), _NT, preferred_element_type=F32)

        def quarter_pair(base):
            ref = gc[base + h4 - 1:base + h4, :]
            qq = (qs[base + h4:base + h2] * jnp.exp(gc[base + h4:base + h2] - ref)).astype(BF16)
            kk = (k[base:base + h4] * jnp.exp(ref - gc[base:base + h4])).astype(BF16)
            return qq, kk

        q2a, k2a = quarter_pair(0)
        q2b, k2b = quarter_pair(h2)
        zeros34 = jnp.zeros((h2 + h4, HEAD_DIM), BF16)
        q2 = jnp.concatenate([jnp.concatenate([zeros4, q2a, zeros2], axis=0),
                              jnp.concatenate([zeros34, q2b], axis=0)], axis=1)
        k2 = jnp.concatenate([jnp.concatenate([k2a, zeros34], axis=0),
                              jnp.concatenate([zeros2, k2b, zeros4], axis=0)], axis=1)
        s2 = lax.dot_general(q2, k2, _NT, preferred_element_type=F32)

        ref3 = jnp.concatenate([bcast(gc[n * h4 + h8 - 1:n * h4 + h8, :], h4) for n in range(4)], axis=0)
        d3 = gc - ref3
        q3 = (qs * jnp.exp(d3)).astype(BF16)
        k3 = (k * jnp.exp(-d3)).astype(BF16)
        s3 = lax.dot_general(q3, k3, _NT, preferred_element_type=F32)

        scores = (s1 + s2 + jnp.where(level3_mask, s3, 0.0)).astype(BF16)

        g_last = gc[c - 1:c, :]
        q_dec = (qs * jnp.exp(gc)).astype(BF16)
        k_end = (k * jnp.exp(g_last - gc)).astype(BF16)
        st = st_ref[g]
        o = (jnp.dot(scores, v, preferred_element_type=F32)
             + lax.dot_general(q_dec, st.astype(BF16), _NT, preferred_element_type=F32))
        st_ref[g] = st * jnp.exp(g_last) + lax.dot_general(v, k_end, _TN, preferred_element_type=F32)

        o = o * lax.rsqrt(jnp.mean(o * o, axis=-1, keepdims=True) + EPS)
        y_ref[rows, cols] = (o * o_gain_all[:, cols] * _silu(gate)).astype(y_ref.dtype)

    def body(n, carry):
        rows = pl.ds(pl.multiple_of(n * c, c), c)
        for g in range(group):
            one_head(rows, g)
        return carry

    lax.fori_loop(0, tile // c, body, 0)


def _hgrn_mixer(proj, lb_logits, o_norm, *, layer, batch, seq, tile=512, group=4):
    m, four_inner = proj.shape
    d_inner = four_inner // 4
    heads = d_inner // HEAD_DIM
    ng = heads // group
    nt = seq // tile
    width = group * HEAD_DIM

    def stream(idx):
        return pl.BlockSpec((tile, width), lambda b, h, t: (b * nt + t, idx * ng + h))

    rows = lb_logits.shape[0]
    return pl.pallas_call(
        functools.partial(_hgrn_kernel, layer=layer, tile=tile, group=group),
        grid=(batch, ng, nt),
        in_specs=[stream(0), stream(1), stream(2), stream(3),
                  pl.BlockSpec((rows, width), lambda b, h, t: (0, h)),
                  pl.BlockSpec((1, width), lambda b, h, t: (0, h))],
        out_specs=pl.BlockSpec((tile, width), lambda b, h, t: (b * nt + t, h)),
        out_shape=jax.ShapeDtypeStruct((m, d_inner), BF16),
        scratch_shapes=[pltpu.VMEM((group, HEAD_DIM, HEAD_DIM), F32)],
        compiler_params=_params("arbitrary", "arbitrary", "arbitrary"),
        name="hgrn2_mixer",
    )(proj, proj, proj, proj, lb_logits, o_norm.reshape(1, d_inner))


def _fgate_kernel(wt_ref, h_ref, b_ref, c_ref, carry_ref, *, bs):
    @pl.when(pl.program_id(1) == 0)
    def _():
        carry_ref[...] = jnp.zeros_like(carry_ref)

    z = lax.dot_general(wt_ref[...], h_ref[...], _NT, preferred_element_type=F32) + b_ref[...]
    ls = jnp.minimum(z, 0.0) - jnp.log1p(jnp.exp(-jnp.abs(z)))
    r = lax.broadcasted_iota(jnp.int32, (bs, bs), 0)
    s = lax.broadcasted_iota(jnp.int32, (bs, bs), 1)
    upper = (r <= s).astype(F32)
    cs = jnp.dot(ls, upper, precision=lax.Precision.HIGHEST, preferred_element_type=F32) + carry_ref[...]
    c_ref[0] = cs * LOG2E
    carry_ref[...] = cs[:, bs - 1:bs]


def _forget_cumsum(h, w_f_t, b_f, *, batch, seq, bs=512):
    heads, d = w_f_t.shape
    nb = seq // bs
    return pl.pallas_call(
        functools.partial(_fgate_kernel, bs=bs),
        grid=(batch, nb),
        in_specs=[pl.BlockSpec((heads, d), lambda b, t: (0, 0)),
                  pl.BlockSpec((bs, d), lambda b, t: (b * nb + t, 0)),
                  pl.BlockSpec((heads, 1), lambda b, t: (0, 0))],
        out_specs=pl.BlockSpec((1, heads, bs), lambda b, t: (b, 0, t)),
        out_shape=jax.ShapeDtypeStruct((batch, heads, seq), F32),
        scratch_shapes=[pltpu.VMEM((heads, 1), F32)],
        compiler_params=_params("arbitrary", "arbitrary"),
        name="forget_cumsum",
    )(w_f_t, h, b_f.reshape(heads, 1))


def _fox_kernel(q_ref, k_ref, v_ref, g_ref, c_ref, y_ref, sa_ref, sb_ref, m_ref, l_ref, acc_ref, *, blk, group):
    i = pl.program_id(2)
    m_ref[...] = jnp.full_like(m_ref, -jnp.inf)
    l_ref[...] = jnp.zeros_like(l_ref)
    acc_ref[...] = jnp.zeros_like(acc_ref)
    nlane = blk // HEAD_DIM

    def head_cols(g):
        return slice(g * HEAD_DIM, (g + 1) * HEAD_DIM)

    def key_rows(j):
        return pl.ds(pl.multiple_of(j * blk, blk), blk)

    def logits(j, s_ref):
        for g in range(group):
            qk = lax.dot_general(q_ref[:, head_cols(g)], k_ref[key_rows(j), head_cols(g)], _NT,
                                 preferred_element_type=F32)
            s_ref[g] = qk - c_ref[g, pl.ds(j, 1), :]

    def softmax_pv(j, s_ref, masked):
        for g in range(group):
            sc = s_ref[g]
            if masked:
                r = lax.broadcasted_iota(jnp.int32, (blk, blk), 0)
                s = lax.broadcasted_iota(jnp.int32, (blk, blk), 1)
                sc = jnp.where(s <= r, sc, -jnp.inf)
            m_prev = m_ref[g]
            m_new = jnp.maximum(m_prev, jnp.max(sc, axis=-1, keepdims=True))
            alpha = jnp.exp2(m_prev - m_new)
            p = jnp.exp2(sc - jnp.concatenate([m_new] * nlane, axis=-1))
            p_part = p[:, :HEAD_DIM]
            for t in range(1, nlane):
                p_part = p_part + p[:, t * HEAD_DIM:(t + 1) * HEAD_DIM]
            l_ref[g] = alpha * l_ref[g] + p_part
            acc_ref[g] = alpha * acc_ref[g] + jnp.dot(p.astype(BF16), v_ref[key_rows(j), head_cols(g)],
                                                      preferred_element_type=F32)
            m_ref[g] = m_new

    logits(0, sa_ref)

    def body(jj, carry):
        logits(2 * jj + 1, sb_ref)
        softmax_pv(2 * jj, sa_ref, False)
        logits(2 * jj + 2, sa_ref)
        softmax_pv(2 * jj + 1, sb_ref, False)
        return carry

    lax.fori_loop(0, i // 2, body, 0)

    @pl.when(i % 2 == 0)
    def _():
        softmax_pv(i, sa_ref, True)

    @pl.when(i % 2 == 1)
    def _():
        logits(i, sb_ref)
        softmax_pv(i - 1, sa_ref, False)
        softmax_pv(i, sb_ref, True)

    for g in range(group):
        cols = slice(g * HEAD_DIM, (g + 1) * HEAD_DIM)
        o = acc_ref[g] / jnp.sum(l_ref[g], axis=-1, keepdims=True)
        y_ref[:, cols] = (o * _silu(g_ref[:, cols].astype(F32))).astype(y_ref.dtype)


def _fox_attention(proj, c, *, batch, seq, blk=512, group=2):
    m = proj.shape[0]
    d_inner = proj.shape[1] // 4
    heads = d_inner // HEAD_DIM
    ng = heads // group
    nq = seq // blk
    width = group * HEAD_DIM
    c_blocks = c.reshape(batch * heads, nq, blk)

    return pl.pallas_call(
        functools.partial(_fox_kernel, blk=blk, group=group),
        grid=(batch, ng, nq),
        in_specs=[pl.BlockSpec((blk, width), lambda b, h, i: (b * nq + i, h)),
                  pl.BlockSpec((seq, width), lambda b, h, i: (b, ng + h)),
                  pl.BlockSpec((seq, width), lambda b, h, i: (b, 2 * ng + h)),
                  pl.BlockSpec((blk, width), lambda b, h, i: (b * nq + i, 3 * ng + h)),
                  pl.BlockSpec((group, nq, blk), lambda b, h, i: (b * ng + h, 0, 0))],
        out_specs=pl.BlockSpec((blk, width), lambda b, h, i: (b * nq + i, h)),
        out_shape=jax.ShapeDtypeStruct((m, d_inner), BF16),
        scratch_shapes=[pltpu.VMEM((group, blk, blk), F32),
                        pltpu.VMEM((group, blk, blk), F32),
                        pltpu.VMEM((group, blk, HEAD_DIM), F32),
                        pltpu.VMEM((group, blk, HEAD_DIM), F32),
                        pltpu.VMEM((group, blk, HEAD_DIM), F32)],
        compiler_params=_params("arbitrary", "arbitrary", "arbitrary"),
        name="fox_attention",
    )(proj, proj, proj, proj, c_blocks)


def kernel(x, norm_w, w_in_a, lb_logits, o_norm_a, w_out_a, w_in_b, b_f, w_out_b, final_norm):
    batch, seq, d_model = x.shape
    depth = norm_w.shape[0]
    xr = x.reshape(batch * seq, d_model)
    for layer in range(depth):
        h = _rmsnorm(xr, norm_w[layer], BF16)
        j = layer // 2
        if layer % 2 == 0:
            d_inner = w_out_a.shape[1]
            proj = _in_proj(h, w_in_a[j].astype(BF16), 4 * d_inner)
            y = _hgrn_mixer(proj, lb_logits, o_norm_a[j], layer=j, batch=batch, seq=seq)
            xr = _out_proj(y, w_out_a[j].astype(BF16), xr)
        else:
            d_inner = w_out_b.shape[1]
            w_b = w_in_b[j].astype(BF16)
            proj = _in_proj(h, w_b, 4 * d_inner, scaled_cols=d_inner, scale=LOG2E / math.sqrt(HEAD_DIM))
            c = _forget_cumsum(h, w_b[:, 4 * d_inner:].T, b_f[j], batch=batch, seq=seq)
            y = _fox_attention(proj, c, batch=batch, seq=seq)
            xr = _out_proj(y, w_out_b[j].astype(BF16), xr)
    return _rmsnorm(xr, final_norm, x.dtype).reshape(batch, seq, d_model)
```

```python
import functools
import math

import jax
import jax.numpy as jnp
from jax import lax
from jax.experimental import pallas as pl
from jax.experimental.pallas import tpu as pltpu

F32 = jnp.float32
BF16 = jnp.bfloat16

EPS = 1e-6
HEAD_DIM = 128
HGRN_BLOCK = 16
HGRN_CHUNK = 8 * HGRN_BLOCK
VMEM_LIMIT = 56 * 1024 * 1024
LOG2E = math.log2(math.e)

_NT = (((1,), (1,)), ((), ()))
_TN = (((0,), (0,)), ((), ()))


def _params(*sem):
    return pltpu.CompilerParams(dimension_semantics=sem, vmem_limit_bytes=VMEM_LIMIT)


def _rmsnorm_kernel(x_ref, w_ref, o_ref):
    x = x_ref[...]
    y = x * lax.rsqrt(jnp.mean(x * x, axis=-1, keepdims=True) + EPS)
    o_ref[...] = (y * w_ref[...]).astype(o_ref.dtype)


def _rmsnorm(x, w, out_dtype, bm=256):
    m, d = x.shape
    return pl.pallas_call(
        _rmsnorm_kernel,
        grid=(m // bm,),
        in_specs=[pl.BlockSpec((bm, d), lambda i: (i, 0)),
                  pl.BlockSpec((1, d), lambda i: (0, 0))],
        out_specs=pl.BlockSpec((bm, d), lambda i: (i, 0)),
        out_shape=jax.ShapeDtypeStruct((m, d), out_dtype),
        compiler_params=_params("arbitrary"),
        name="rmsnorm",
    )(x, w.reshape(1, d))


def _mm_kernel(a_ref, w_ref, o_ref, *wb_ref, scaled_blocks, scale):
    if wb_ref:
        @pl.when(pl.program_id(1) == 0)
        def _():
            wb_ref[0][...] = w_ref[...].astype(BF16)
        w = wb_ref[0][...]
    else:
        w = w_ref[...]
    acc = jnp.dot(a_ref[...], w, preferred_element_type=F32)
    if scaled_blocks:
        acc = acc * jnp.where(pl.program_id(0) < scaled_blocks, scale, 1.0)
    o_ref[...] = acc.astype(o_ref.dtype)


def _in_proj(a, w, n, scaled_cols=0, scale=1.0, bn=1024):
    m, k = a.shape
    in_kernel_cast = w.dtype != BF16
    bm = 512 if in_kernel_cast else 1024
    return pl.pallas_call(
        functools.partial(_mm_kernel, scaled_blocks=scaled_cols // bn, scale=scale),
        grid=(n // bn, m // bm),
        in_specs=[pl.BlockSpec((bm, k), lambda j, i: (i, 0)),
                  pl.BlockSpec((k, bn), lambda j, i: (0, j))],
        out_specs=pl.BlockSpec((bm, bn), lambda j, i: (i, j)),
        out_shape=jax.ShapeDtypeStruct((m, n), BF16),
        scratch_shapes=[pltpu.VMEM((k, bn), BF16)] if in_kernel_cast else [],
        compiler_params=_params("arbitrary", "arbitrary"),
        name="in_proj",
    )(a, w)


def _mm_res_kernel(a_ref, b_ref, r_ref, o_ref):
    o_ref[...] = r_ref[...] + jnp.dot(a_ref[...], b_ref[...], preferred_element_type=F32)


def _out_proj(a, b, res, bm=512, bn=512):
    m, k = a.shape
    n = b.shape[1]
    return pl.pallas_call(
        _mm_res_kernel,
        grid=(n // bn, m // bm),
        in_specs=[pl.BlockSpec((bm, k), lambda j, i: (i, 0)),
                  pl.BlockSpec((k, bn), lambda j, i: (0, j)),
                  pl.BlockSpec((bm, bn), lambda j, i: (i, j))],
        out_specs=pl.BlockSpec((bm, bn), lambda j, i: (i, j)),
        out_shape=jax.ShapeDtypeStruct((m, n), F32),
        compiler_params=_params("arbitrary", "arbitrary"),
        name="out_proj",
    )(a, b, res)


def _silu(x):
    return x * jax.nn.sigmoid(x)


def _hgrn_kernel(q_ref, f_ref, i_ref, g_ref, lbl_ref, on_ref, y_ref, st_ref, *, layer, tile, group):
    c = HGRN_CHUNK
    h2, h4, h8 = c // 2, c // 4, c // 8

    @pl.when(pl.program_id(2) == 0)
    def _():
        st_ref[...] = jnp.zeros_like(st_ref)

    lbl = lbl_ref[...]
    e = jnp.exp(lbl - jnp.max(lbl, axis=0, keepdims=True))
    lb_all = jnp.sum(e[:layer + 1], axis=0, keepdims=True) / jnp.sum(e, axis=0, keepdims=True)
    o_gain_all = on_ref[...]

    r = lax.broadcasted_iota(jnp.int32, (c, c), 0)
    s = lax.broadcasted_iota(jnp.int32, (c, c), 1)
    tri = (s <= r).astype(BF16)
    tri3 = jnp.concatenate([tri, tri, tri], axis=1)
    same_quarter = (r // h4) == (s // h4)
    level3_mask = jnp.logical_and(same_quarter, s <= r)

    width = group * HEAD_DIM
    heads = [slice(g * HEAD_DIM, (g + 1) * HEAD_DIM) for g in range(group)]

    def zeros(n):
        return jnp.zeros((n, width), BF16)

    def body(n, carry):
        rows = pl.ds(pl.multiple_of(n * c, c), c)
        q = q_ref[rows, :].astype(F32)
        f = f_ref[rows, :].astype(F32)
        v = i_ref[rows, :]
        gate = g_ref[rows, :].astype(F32)

        forget = lb_all + (1.0 - lb_all) * jax.nn.sigmoid(f)
        k = 1.0 - forget
        log_f = jnp.log(forget)
        qs = _silu(q)

        hi = log_f.astype(BF16)
        rem = log_f - hi.astype(F32)
        mid = rem.astype(BF16)
        lo = (rem - mid.astype(F32)).astype(BF16)
        gc = jnp.dot(tri3, jnp.concatenate([hi, mid, lo], axis=0), preferred_element_type=F32)

        ref1 = gc[h2 - 1:h2, :]
        q1 = jnp.concatenate([zeros(h2), (qs[h2:] * jnp.exp(gc[h2:] - ref1)).astype(BF16)], axis=0)
        k1 = jnp.concatenate([(k[:h2] * jnp.exp(ref1 - gc[:h2])).astype(BF16), zeros(h2)], axis=0)

        def quarter_pair(base):
            ref = gc[base + h4 - 1:base + h4, :]
            qq = (qs[base + h4:base + h2] * jnp.exp(gc[base + h4:base + h2] - ref)).astype(BF16)
            kk = (k[base:base + h4] * jnp.exp(ref - gc[base:base + h4])).astype(BF16)
            return qq, kk

        q2a, k2a = quarter_pair(0)
        q2b, k2b = quarter_pair(h2)
        q2a = jnp.concatenate([zeros(h4), q2a, zeros(h2)], axis=0)
        q2b = jnp.concatenate([zeros(h2 + h4), q2b], axis=0)
        k2a = jnp.concatenate([k2a, zeros(h2 + h4)], axis=0)
        k2b = jnp.concatenate([zeros(h2), k2b, zeros(h4)], axis=0)

        ref3 = jnp.concatenate(
            [jnp.broadcast_to(gc[m * h4 + h8 - 1:m * h4 + h8, :], (h4, width)) for m in range(4)], axis=0)
        d3 = gc - ref3
        q3 = (qs * jnp.exp(d3)).astype(BF16)
        k3 = (k * jnp.exp(-d3)).astype(BF16)

        g_last = gc[c - 1:c, :]
        q_dec = (qs * jnp.exp(gc)).astype(BF16)
        k_end = (k * jnp.exp(g_last - gc)).astype(BF16)
        decay = jnp.exp(g_last)
        out_scale = o_gain_all * _silu(gate)

        def nt(a, b):
            return lax.dot_general(a, b, _NT, preferred_element_type=F32)

        s1 = [nt(q1[:, h], k1[:, h]) for h in heads]
        s2 = [nt(jnp.concatenate([q2a[:, h], q2b[:, h]], axis=1),
                 jnp.concatenate([k2a[:, h], k2b[:, h]], axis=1)) for h in heads]
        s3 = [nt(q3[:, h], k3[:, h]) for h in heads]
        scores = [(s1[g] + s2[g] + jnp.where(level3_mask, s3[g], 0.0)).astype(BF16) for g in range(group)]

        st = [st_ref[g] for g in range(group)]
        o = [jnp.dot(scores[g], v[:, h], preferred_element_type=F32) + nt(q_dec[:, h], st[g].astype(BF16))
             for g, h in enumerate(heads)]
        kv = [lax.dot_general(v[:, h], k_end[:, h], _TN, preferred_element_type=F32) for h in heads]
        for g, h in enumerate(heads):
            st_ref[g] = st[g] * decay[:, h] + kv[g]
            og = o[g] * lax.rsqrt(jnp.mean(o[g] * o[g], axis=-1, keepdims=True) + EPS)
            y_ref[rows, h] = (og * out_scale[:, h]).astype(y_ref.dtype)
        return carry

    lax.fori_loop(0, tile // c, body, 0)


def _hgrn_mixer(proj, lb_logits, o_norm, *, layer, batch, seq, tile=512, group=4):
    m, four_inner = proj.shape
    d_inner = four_inner // 4
    heads = d_inner // HEAD_DIM
    ng = heads // group
    nt = seq // tile
    width = group * HEAD_DIM

    def stream(idx):
        return pl.BlockSpec((tile, width), lambda b, h, t: (b * nt + t, idx * ng + h))

    rows = lb_logits.shape[0]
    return pl.pallas_call(
        functools.partial(_hgrn_kernel, layer=layer, tile=tile, group=group),
        grid=(batch, ng, nt),
        in_specs=[stream(0), stream(1), stream(2), stream(3),
                  pl.BlockSpec((rows, width), lambda b, h, t: (0, h)),
                  pl.BlockSpec((1, width), lambda b, h, t: (0, h))],
        out_specs=pl.BlockSpec((tile, width), lambda b, h, t: (b * nt + t, h)),
        out_shape=jax.ShapeDtypeStruct((m, d_inner), BF16),
        scratch_shapes=[pltpu.VMEM((group, HEAD_DIM, HEAD_DIM), F32)],
        compiler_params=_params("arbitrary", "arbitrary", "arbitrary"),
        name="hgrn2_mixer",
    )(proj, proj, proj, proj, lb_logits, o_norm.reshape(1, d_inner))


def _fgate_kernel(w_ref, h_ref, b_ref, c_ref, carry_ref, *, bs, heads):
    @pl.when(pl.program_id(1) == 0)
    def _():
        carry_ref[...] = jnp.zeros_like(carry_ref)

    zt = jnp.dot(h_ref[...], w_ref[...].astype(BF16), preferred_element_type=F32).T
    z = zt[:heads] + b_ref[...]
    ls = jnp.minimum(z, 0.0) - jnp.log1p(jnp.exp(-jnp.abs(z)))
    r = lax.broadcasted_iota(jnp.int32, (bs, bs), 0)
    s = lax.broadcasted_iota(jnp.int32, (bs, bs), 1)
    upper = (r <= s).astype(F32)
    cs = jnp.dot(ls, upper, precision=lax.Precision.HIGHEST, preferred_element_type=F32) + carry_ref[...]
    c_ref[0] = cs * LOG2E
    carry_ref[...] = cs[:, bs - 1:bs]


def _forget_cumsum(h, w_f, b_f, *, batch, seq, bs=512):
    d, heads = w_f.shape
    lanes = -(-heads // HEAD_DIM) * HEAD_DIM
    w_f = jnp.pad(w_f, ((0, 0), (0, lanes - heads)))
    nb = seq // bs
    return pl.pallas_call(
        functools.partial(_fgate_kernel, bs=bs, heads=heads),
        grid=(batch, nb),
        in_specs=[pl.BlockSpec((d, lanes), lambda b, t: (0, 0)),
                  pl.BlockSpec((bs, d), lambda b, t: (b * nb + t, 0)),
                  pl.BlockSpec((heads, 1), lambda b, t: (0, 0))],
        out_specs=pl.BlockSpec((1, heads, bs), lambda b, t: (b, 0, t)),
        out_shape=jax.ShapeDtypeStruct((batch, heads, seq), F32),
        scratch_shapes=[pltpu.VMEM((heads, 1), F32)],
        compiler_params=_params("arbitrary", "arbitrary"),
        name="forget_cumsum",
    )(w_f, h, b_f.reshape(heads, 1))


def _fox_kernel(q_ref, k_ref, v_ref, g_ref, c_ref, y_ref, sa_ref, sb_ref, m_ref, l_ref, acc_ref, *, blk, group):
    i = pl.program_id(2)
    m_ref[...] = jnp.full_like(m_ref, -jnp.inf)
    l_ref[...] = jnp.zeros_like(l_ref)
    acc_ref[...] = jnp.zeros_like(acc_ref)
    nlane = blk // HEAD_DIM

    def head_cols(g):
        return slice(g * HEAD_DIM, (g + 1) * HEAD_DIM)

    def key_rows(j):
        return pl.ds(pl.multiple_of(j * blk, blk), blk)

    def logits(j, s_ref):
        for g in range(group):
            qk = lax.dot_general(q_ref[:, head_cols(g)], k_ref[key_rows(j), head_cols(g)], _NT,
                                 preferred_element_type=F32)
            s_ref[g] = qk - c_ref[g, pl.ds(j, 1), :]

    def softmax_pv(j, s_ref, masked):
        for g in range(group):
            sc = s_ref[g]
            if masked:
                r = lax.broadcasted_iota(jnp.int32, (blk, blk), 0)
                s = lax.broadcasted_iota(jnp.int32, (blk, blk), 1)
                sc = jnp.where(s <= r, sc, -jnp.inf)
            m_prev = m_ref[g]
            m_new = jnp.maximum(m_prev, jnp.max(sc, axis=-1, keepdims=True))
            alpha = jnp.exp2(m_prev - m_new)
            p = jnp.exp2(sc - jnp.concatenate([m_new] * nlane, axis=-1))
            p_part = p[:, :HEAD_DIM]
            for t in range(1, nlane):
                p_part = p_part + p[:, t * HEAD_DIM:(t + 1) * HEAD_DIM]
            l_ref[g] = alpha * l_ref[g] + p_part
            acc_ref[g] = alpha * acc_ref[g] + jnp.dot(p.astype(BF16), v_ref[key_rows(j), head_cols(g)],
                                                      preferred_element_type=F32)
            m_ref[g] = m_new

    logits(0, sa_ref)

    def body(jj, carry):
        logits(2 * jj + 1, sb_ref)
        softmax_pv(2 * jj, sa_ref, False)
        logits(2 * jj + 2, sa_ref)
        softmax_pv(2 * jj + 1, sb_ref, False)
        return carry

    lax.fori_loop(0, i // 2, body, 0)

    @pl.when(i % 2 == 0)
    def _():
        softmax_pv(i, sa_ref, True)

    @pl.when(i % 2 == 1)
    def _():
        logits(i, sb_ref)
        softmax_pv(i - 1, sa_ref, False)
        softmax_pv(i, sb_ref, True)

    for g in range(group):
        cols = slice(g * HEAD_DIM, (g + 1) * HEAD_DIM)
        o = acc_ref[g] / jnp.sum(l_ref[g], axis=-1, keepdims=True)
        y_ref[:, cols] = (o * _silu(g_ref[:, cols].astype(F32))).astype(y_ref.dtype)


def _fox_attention(proj, c, *, batch, seq, blk=512, group=2):
    m = proj.shape[0]
    d_inner = proj.shape[1] // 4
    heads = d_inner // HEAD_DIM
    ng = heads // group
    nq = seq // blk
    width = group * HEAD_DIM
    c_blocks = c.reshape(batch * heads, nq, blk)

    return pl.pallas_call(
        functools.partial(_fox_kernel, blk=blk, group=group),
        grid=(batch, ng, nq),
        in_specs=[pl.BlockSpec((blk, width), lambda b, h, i: (b * nq + i, h)),
                  pl.BlockSpec((seq, width), lambda b, h, i: (b, ng + h)),
                  pl.BlockSpec((seq, width), lambda b, h, i: (b, 2 * ng + h)),
                  pl.BlockSpec((blk, width), lambda b, h, i: (b * nq + i, 3 * ng + h)),
                  pl.BlockSpec((group, nq, blk), lambda b, h, i: (b * ng + h, 0, 0))],
        out_specs=pl.BlockSpec((blk, width), lambda b, h, i: (b * nq + i, h)),
        out_shape=jax.ShapeDtypeStruct((m, d_inner), BF16),
        scratch_shapes=[pltpu.VMEM((group, blk, blk), F32),
                        pltpu.VMEM((group, blk, blk), F32),
                        pltpu.VMEM((group, blk, HEAD_DIM), F32),
                        pltpu.VMEM((group, blk, HEAD_DIM), F32),
                        pltpu.VMEM((group, blk, HEAD_DIM), F32)],
        compiler_params=_params("arbitrary", "arbitrary", "arbitrary"),
        name="fox_attention",
    )(proj, proj, proj, proj, c_blocks)


def kernel(x, norm_w, w_in_a, lb_logits, o_norm_a, w_out_a, w_in_b, b_f, w_out_b, final_norm):
    batch, seq, d_model = x.shape
    depth = norm_w.shape[0]
    xr = x.reshape(batch * seq, d_model)
    for layer in range(depth):
        h = _rmsnorm(xr, norm_w[layer], BF16)
        j = layer // 2
        if layer % 2 == 0:
            d_inner = w_out_a.shape[1]
            proj = _in_proj(h, w_in_a[j], 4 * d_inner)
            y = _hgrn_mixer(proj, lb_logits, o_norm_a[j], layer=j, batch=batch, seq=seq)
            xr = _out_proj(y, w_out_a[j].astype(BF16), xr)
        else:
            d_inner = w_out_b.shape[1]
            proj = _in_proj(h, w_in_b[j], 4 * d_inner, scaled_cols=d_inner, scale=LOG2E / math.sqrt(HEAD_DIM))
            c = _forget_cumsum(h, w_in_b[j][:, 4 * d_inner:], b_f[j], batch=batch, seq=seq)
            y = _fox_attention(proj, c, batch=batch, seq=seq)
            xr = _out_proj(y, w_out_b[j].astype(BF16), xr)
    return _rmsnorm(xr, final_norm, x.dtype).reshape(batch, seq, d_model)
```

```python
import functools
import math

import jax
import jax.numpy as jnp
from jax import lax
from jax.experimental import pallas as pl
from jax.experimental.pallas import tpu as pltpu

F32 = jnp.float32
BF16 = jnp.bfloat16

EPS = 1e-6
HEAD_DIM = 128
HGRN_BLOCK = 16
HGRN_CHUNK = 8 * HGRN_BLOCK
VMEM_LIMIT = 56 * 1024 * 1024
LOG2E = math.log2(math.e)

_NT = (((1,), (1,)), ((), ()))
_TN = (((0,), (0,)), ((), ()))


def _params(*sem):
    return pltpu.CompilerParams(dimension_semantics=sem, vmem_limit_bytes=VMEM_LIMIT)


def _rmsnorm_kernel(x_ref, w_ref, o_ref):
    x = x_ref[...]
    y = x * lax.rsqrt(jnp.mean(x * x, axis=-1, keepdims=True) + EPS)
    o_ref[...] = (y * w_ref[...]).astype(o_ref.dtype)


def _rmsnorm(x, w, out_dtype, bm=256):
    m, d = x.shape
    return pl.pallas_call(
        _rmsnorm_kernel,
        grid=(m // bm,),
        in_specs=[pl.BlockSpec((bm, d), lambda i: (i, 0)),
                  pl.BlockSpec((1, d), lambda i: (0, 0))],
        out_specs=pl.BlockSpec((bm, d), lambda i: (i, 0)),
        out_shape=jax.ShapeDtypeStruct((m, d), out_dtype),
        compiler_params=_params("arbitrary"),
        name="rmsnorm",
    )(x, w.reshape(1, d))


def _mm_kernel(a_ref, w_ref, o_ref, wb_ref, *, w_transposed, scaled_blocks, scale):
    @pl.when(pl.program_id(1) == 0)
    def _():
        wb_ref[...] = w_ref[...].astype(BF16)

    if w_transposed:
        acc = lax.dot_general(a_ref[...], wb_ref[...], _NT, preferred_element_type=F32)
    else:
        acc = jnp.dot(a_ref[...], wb_ref[...], preferred_element_type=F32)
    if scaled_blocks:
        acc = acc * jnp.where(pl.program_id(0) < scaled_blocks, scale, 1.0)
    o_ref[...] = acc.astype(o_ref.dtype)


def _in_proj(a, w, n, w_transposed=False, scaled_cols=0, scale=1.0, bm=512, bn=1024):
    m, k = a.shape
    if w_transposed:
        w_spec = pl.BlockSpec((bn, k), lambda j, i: (j, 0))
    else:
        w_spec = pl.BlockSpec((k, bn), lambda j, i: (0, j))
    return pl.pallas_call(
        functools.partial(_mm_kernel, w_transposed=w_transposed, scaled_blocks=scaled_cols // bn, scale=scale),
        grid=(n // bn, m // bm),
        in_specs=[pl.BlockSpec((bm, k), lambda j, i: (i, 0)), w_spec],
        out_specs=pl.BlockSpec((bm, bn), lambda j, i: (i, j)),
        out_shape=jax.ShapeDtypeStruct((m, n), BF16),
        scratch_shapes=[pltpu.VMEM(w_spec.block_shape, BF16)],
        compiler_params=_params("arbitrary", "arbitrary"),
        name="in_proj",
    )(a, w)


def _mm_res_kernel(a_ref, b_ref, r_ref, o_ref):
    o_ref[...] = r_ref[...] + jnp.dot(a_ref[...], b_ref[...], preferred_element_type=F32)


def _out_proj(a, b, res, bm=512, bn=512):
    m, k = a.shape
    n = b.shape[1]
    return pl.pallas_call(
        _mm_res_kernel,
        grid=(n // bn, m // bm),
        in_specs=[pl.BlockSpec((bm, k), lambda j, i: (i, 0)),
                  pl.BlockSpec((k, bn), lambda j, i: (0, j)),
                  pl.BlockSpec((bm, bn), lambda j, i: (i, j))],
        out_specs=pl.BlockSpec((bm, bn), lambda j, i: (i, j)),
        out_shape=jax.ShapeDtypeStruct((m, n), F32),
        compiler_params=_params("arbitrary", "arbitrary"),
        name="out_proj",
    )(a, b, res)


def _silu(x):
    half = 0.5 * x
    return half + half * jnp.tanh(half)


def _hgrn_kernel(q_ref, f_ref, i_ref, g_ref, lbl_ref, on_ref, y_ref, st_ref, *, layer, tile, group):
    c = HGRN_CHUNK
    h2, h4, h8 = c // 2, c // 4, c // 8

    @pl.when(pl.program_id(2) == 0)
    def _():
        st_ref[...] = jnp.zeros_like(st_ref)

    lbl = lbl_ref[...]
    e = jnp.exp(lbl - jnp.max(lbl, axis=0, keepdims=True))
    lb_all = jnp.sum(e[:layer + 1], axis=0, keepdims=True) / jnp.sum(e, axis=0, keepdims=True)
    o_gain_all = on_ref[...]
    f_half = 0.5 * (1.0 - lb_all)
    f_mid = lb_all + f_half

    r = lax.broadcasted_iota(jnp.int32, (c, c), 0)
    s = lax.broadcasted_iota(jnp.int32, (c, c), 1)
    tri = (s <= r).astype(BF16)
    tri3 = jnp.concatenate([tri, tri, tri], axis=1)
    same_quarter = (r // h4) == (s // h4)
    level3_mask = jnp.logical_and(same_quarter, s <= r)

    width = group * HEAD_DIM
    heads = [slice(g * HEAD_DIM, (g + 1) * HEAD_DIM) for g in range(group)]

    def zeros(n):
        return jnp.zeros((n, width), BF16)

    def body(n, carry):
        rows = pl.ds(pl.multiple_of(n * c, c), c)
        q = q_ref[rows, :].astype(F32)
        f = f_ref[rows, :].astype(F32)
        v = i_ref[rows, :]
        gate = g_ref[rows, :].astype(F32)

        forget = f_mid + f_half * jnp.tanh(0.5 * f)
        k = 1.0 - forget
        log_f = jnp.log2(forget)
        qs = _silu(q)

        hi = log_f.astype(BF16)
        rem = log_f - hi.astype(F32)
        mid = rem.astype(BF16)
        lo = (rem - mid.astype(F32)).astype(BF16)
        gc = jnp.dot(tri3, jnp.concatenate([hi, mid, lo], axis=0), preferred_element_type=F32)

        ref1 = gc[h2 - 1:h2, :]
        q1 = jnp.concatenate([zeros(h2), (qs[h2:] * jnp.exp2(gc[h2:] - ref1)).astype(BF16)], axis=0)
        k1 = jnp.concatenate([(k[:h2] * jnp.exp2(ref1 - gc[:h2])).astype(BF16), zeros(h2)], axis=0)

        def quarter_pair(base):
            ref = gc[base + h4 - 1:base + h4, :]
            qq = (qs[base + h4:base + h2] * jnp.exp2(gc[base + h4:base + h2] - ref)).astype(BF16)
            kk = (k[base:base + h4] * jnp.exp2(ref - gc[base:base + h4])).astype(BF16)
            return qq, kk

        q2a, k2a = quarter_pair(0)
        q2b, k2b = quarter_pair(h2)
        q2a = jnp.concatenate([zeros(h4), q2a, zeros(h2)], axis=0)
        q2b = jnp.concatenate([zeros(h2 + h4), q2b], axis=0)
        k2a = jnp.concatenate([k2a, zeros(h2 + h4)], axis=0)
        k2b = jnp.concatenate([zeros(h2), k2b, zeros(h4)], axis=0)

        ref3 = jnp.concatenate(
            [jnp.broadcast_to(gc[m * h4 + h8 - 1:m * h4 + h8, :], (h4, width)) for m in range(4)], axis=0)
        d3 = gc - ref3
        q3 = (qs * jnp.exp2(d3)).astype(BF16)
        k3 = (k * jnp.exp2(-d3)).astype(BF16)

        g_last = gc[c - 1:c, :]
        q_dec = (qs * jnp.exp2(gc)).astype(BF16)
        k_end = (k * jnp.exp2(g_last - gc)).astype(BF16)
        decay = jnp.exp2(g_last)
        out_scale = o_gain_all * _silu(gate)

        def nt(a, b):
            return lax.dot_general(a, b, _NT, preferred_element_type=F32)

        s1 = [nt(q1[:, h], k1[:, h]) for h in heads]
        s2 = [nt(jnp.concatenate([q2a[:, h], q2b[:, h]], axis=1),
                 jnp.concatenate([k2a[:, h], k2b[:, h]], axis=1)) for h in heads]
        s3 = [nt(q3[:, h], k3[:, h]) for h in heads]
        scores = [(s1[g] + s2[g] + jnp.where(level3_mask, s3[g], 0.0)).astype(BF16) for g in range(group)]

        st = [st_ref[g] for g in range(group)]
        o = [jnp.dot(scores[g], v[:, h], preferred_element_type=F32) + nt(q_dec[:, h], st[g].astype(BF16))
             for g, h in enumerate(heads)]
        kv = [lax.dot_general(v[:, h], k_end[:, h], _TN, preferred_element_type=F32) for h in heads]
        for g, h in enumerate(heads):
            st_ref[g] = st[g] * decay[:, h] + kv[g]
            og = o[g] * lax.rsqrt(jnp.mean(o[g] * o[g], axis=-1, keepdims=True) + EPS)
            y_ref[rows, h] = (og * out_scale[:, h]).astype(y_ref.dtype)
        return carry

    lax.fori_loop(0, tile // c, body, 0)


def _hgrn_mixer(proj, lb_logits, o_norm, *, layer, batch, seq, tile=512, group=8):
    m, four_inner = proj.shape
    d_inner = four_inner // 4
    heads = d_inner // HEAD_DIM
    ng = heads // group
    nt = seq // tile
    width = group * HEAD_DIM

    def stream(idx):
        return pl.BlockSpec((tile, width), lambda b, h, t: (b * nt + t, idx * ng + h))

    rows = lb_logits.shape[0]
    return pl.pallas_call(
        functools.partial(_hgrn_kernel, layer=layer, tile=tile, group=group),
        grid=(batch, ng, nt),
        in_specs=[stream(0), stream(1), stream(2), stream(3),
                  pl.BlockSpec((rows, width), lambda b, h, t: (0, h)),
                  pl.BlockSpec((1, width), lambda b, h, t: (0, h))],
        out_specs=pl.BlockSpec((tile, width), lambda b, h, t: (b * nt + t, h)),
        out_shape=jax.ShapeDtypeStruct((m, d_inner), BF16),
        scratch_shapes=[pltpu.VMEM((group, HEAD_DIM, HEAD_DIM), F32)],
        compiler_params=_params("arbitrary", "arbitrary", "arbitrary"),
        name="hgrn2_mixer",
    )(proj, proj, proj, proj, lb_logits, o_norm.reshape(1, d_inner))


def _fgate_kernel(w_ref, h_ref, b_ref, c_ref, carry_ref, *, bs, heads):
    @pl.when(pl.program_id(1) == 0)
    def _():
        carry_ref[...] = jnp.zeros_like(carry_ref)

    zt = jnp.dot(h_ref[...], w_ref[...].astype(BF16), preferred_element_type=F32).T
    z = zt[:heads] + b_ref[...]
    ls = jnp.minimum(z, 0.0) - jnp.log1p(jnp.exp(-jnp.abs(z)))
    r = lax.broadcasted_iota(jnp.int32, (bs, bs), 0)
    s = lax.broadcasted_iota(jnp.int32, (bs, bs), 1)
    upper = (r <= s).astype(F32)
    cs = jnp.dot(ls, upper, precision=lax.Precision.HIGHEST, preferred_element_type=F32) + carry_ref[...]
    c_ref[0] = cs * LOG2E
    carry_ref[...] = cs[:, bs - 1:bs]


def _forget_cumsum(h, w_f, b_f, *, batch, seq, bs=512):
    d, heads = w_f.shape
    lanes = -(-heads // HEAD_DIM) * HEAD_DIM
    w_f = jnp.pad(w_f, ((0, 0), (0, lanes - heads)))
    nb = seq // bs
    return pl.pallas_call(
        functools.partial(_fgate_kernel, bs=bs, heads=heads),
        grid=(batch, nb),
        in_specs=[pl.BlockSpec((d, lanes), lambda b, t: (0, 0)),
                  pl.BlockSpec((bs, d), lambda b, t: (b * nb + t, 0)),
                  pl.BlockSpec((heads, 1), lambda b, t: (0, 0))],
        out_specs=pl.BlockSpec((1, heads, bs), lambda b, t: (b, 0, t)),
        out_shape=jax.ShapeDtypeStruct((batch, heads, seq), F32),
        scratch_shapes=[pltpu.VMEM((heads, 1), F32)],
        compiler_params=_params("arbitrary", "arbitrary"),
        name="forget_cumsum",
    )(w_f, h, b_f.reshape(heads, 1))


def _fox_kernel(q_ref, k_ref, v_ref, g_ref, c_ref, y_ref, sa_ref, sb_ref, m_ref, l_ref, acc_ref, *, blk, group):
    i = pl.program_id(2)
    m_ref[...] = jnp.full_like(m_ref, -jnp.inf)
    l_ref[...] = jnp.zeros_like(l_ref)
    acc_ref[...] = jnp.zeros_like(acc_ref)
    nlane = blk // HEAD_DIM

    def head_cols(g):
        return slice(g * HEAD_DIM, (g + 1) * HEAD_DIM)

    def key_rows(j):
        return pl.ds(pl.multiple_of(j * blk, blk), blk)

    def logits(j, s_ref):
        for g in range(group):
            qk = lax.dot_general(q_ref[:, head_cols(g)], k_ref[key_rows(j), head_cols(g)], _NT,
                                 preferred_element_type=F32)
            s_ref[g] = qk - c_ref[g, pl.ds(j, 1), :]

    def softmax_pv(j, s_ref, masked):
        for g in range(group):
            sc = s_ref[g]
            if masked:
                r = lax.broadcasted_iota(jnp.int32, (blk, blk), 0)
                s = lax.broadcasted_iota(jnp.int32, (blk, blk), 1)
                sc = jnp.where(s <= r, sc, -jnp.inf)
            m_prev = m_ref[g]
            m_new = jnp.maximum(m_prev, jnp.max(sc, axis=-1, keepdims=True))
            alpha = jnp.exp2(m_prev - m_new)
            p = jnp.exp2(sc - jnp.concatenate([m_new] * nlane, axis=-1))
            p_part = p[:, :HEAD_DIM]
            for t in range(1, nlane):
                p_part = p_part + p[:, t * HEAD_DIM:(t + 1) * HEAD_DIM]
            l_ref[g] = alpha * l_ref[g] + p_part
            acc_ref[g] = alpha * acc_ref[g] + jnp.dot(p.astype(BF16), v_ref[key_rows(j), head_cols(g)],
                                                      preferred_element_type=F32)
            m_ref[g] = m_new

    logits(0, sa_ref)

    def body(jj, carry):
        logits(2 * jj + 1, sb_ref)
        softmax_pv(2 * jj, sa_ref, False)
        logits(2 * jj + 2, sa_ref)
        softmax_pv(2 * jj + 1, sb_ref, False)
        return carry

    lax.fori_loop(0, i // 2, body, 0)

    @pl.when(i % 2 == 0)
    def _():
        softmax_pv(i, sa_ref, True)

    @pl.when(i % 2 == 1)
    def _():
        logits(i, sb_ref)
        softmax_pv(i - 1, sa_ref, False)
        softmax_pv(i, sb_ref, True)

    for g in range(group):
        cols = slice(g * HEAD_DIM, (g + 1) * HEAD_DIM)
        o = acc_ref[g] / jnp.sum(l_ref[g], axis=-1, keepdims=True)
        y_ref[:, cols] = (o * _silu(g_ref[:, cols].astype(F32))).astype(y_ref.dtype)


def _fox_attention(proj, c, *, batch, seq, blk=512, group=4):
    m = proj.shape[0]
    d_inner = proj.shape[1] // 4
    heads = d_inner // HEAD_DIM
    ng = heads // group
    nq = seq // blk
    width = group * HEAD_DIM
    c_blocks = c.reshape(batch * heads, nq, blk)

    return pl.pallas_call(
        functools.partial(_fox_kernel, blk=blk, group=group),
        grid=(batch, ng, nq),
        in_specs=[pl.BlockSpec((blk, width), lambda b, h, i: (b * nq + i, h)),
                  pl.BlockSpec((seq, width), lambda b, h, i: (b, ng + h)),
                  pl.BlockSpec((seq, width), lambda b, h, i: (b, 2 * ng + h)),
                  pl.BlockSpec((blk, width), lambda b, h, i: (b * nq + i, 3 * ng + h)),
                  pl.BlockSpec((group, nq, blk), lambda b, h, i: (b * ng + h, 0, 0))],
        out_specs=pl.BlockSpec((blk, width), lambda b, h, i: (b * nq + i, h)),
        out_shape=jax.ShapeDtypeStruct((m, d_inner), BF16),
        scratch_shapes=[pltpu.VMEM((group, blk, blk), F32),
                        pltpu.VMEM((group, blk, blk), F32),
                        pltpu.VMEM((group, blk, HEAD_DIM), F32),
                        pltpu.VMEM((group, blk, HEAD_DIM), F32),
                        pltpu.VMEM((group, blk, HEAD_DIM), F32)],
        compiler_params=_params("arbitrary", "arbitrary", "arbitrary"),
        name="fox_attention",
    )(proj, proj, proj, proj, c_blocks)


def kernel(x, norm_w, w_in_a, lb_logits, o_norm_a, w_out_a, w_in_b, b_f, w_out_b, final_norm):
    batch, seq, d_model = x.shape
    depth = norm_w.shape[0]
    xr = x.reshape(batch * seq, d_model)
    for layer in range(depth):
        h = _rmsnorm(xr, norm_w[layer], BF16)
        j = layer // 2
        if layer % 2 == 0:
            d_inner = w_out_a.shape[1]
            proj = _in_proj(h, w_in_a[j], 4 * d_inner)
            y = _hgrn_mixer(proj, lb_logits, o_norm_a[j], layer=j, batch=batch, seq=seq)
            xr = _out_proj(y, w_out_a[j].astype(BF16), xr)
        else:
            d_inner = w_out_b.shape[1]
            proj = _in_proj(h, w_in_b[j].T, 4 * d_inner, w_transposed=True,
                            scaled_cols=d_inner, scale=LOG2E / math.sqrt(HEAD_DIM))
            c = _forget_cumsum(h, w_in_b[j][:, 4 * d_inner:], b_f[j], batch=batch, seq=seq)
            y = _fox_attention(proj, c, batch=batch, seq=seq)
            xr = _out_proj(y, w_out_b[j].astype(BF16), xr)
    return _rmsnorm(xr, final_norm, x.dtype).reshape(batch, seq, d_model)
```

```python
import functools
import math

import jax
import jax.numpy as jnp
from jax import lax
from jax.experimental import pallas as pl
from jax.experimental.pallas import tpu as pltpu

F32 = jnp.float32
BF16 = jnp.bfloat16

EPS = 1e-6
HEAD_DIM = 128
HGRN_BLOCK = 16
HGRN_CHUNK = 8 * HGRN_BLOCK
VMEM_LIMIT = 56 * 1024 * 1024
LOG2E = math.log2(math.e)

_NT = (((1,), (1,)), ((), ()))
_TN = (((0,), (0,)), ((), ()))


def _params(*sem):
    return pltpu.CompilerParams(dimension_semantics=sem, vmem_limit_bytes=VMEM_LIMIT)


def _rmsnorm_kernel(x_ref, w_ref, o_ref):
    x = x_ref[...]
    y = x * lax.rsqrt(jnp.mean(x * x, axis=-1, keepdims=True) + EPS)
    o_ref[...] = (y * w_ref[...]).astype(o_ref.dtype)


def _rmsnorm(x, w, out_dtype, bm=256):
    m, d = x.shape
    return pl.pallas_call(
        _rmsnorm_kernel,
        grid=(m // bm,),
        in_specs=[pl.BlockSpec((bm, d), lambda i: (i, 0)),
                  pl.BlockSpec((1, d), lambda i: (0, 0))],
        out_specs=pl.BlockSpec((bm, d), lambda i: (i, 0)),
        out_shape=jax.ShapeDtypeStruct((m, d), out_dtype),
        compiler_params=_params("arbitrary"),
        name="rmsnorm",
    )(x, w.reshape(1, d))


def _mm_kernel(a_ref, w_ref, o_ref, wb_ref, *, w_transposed, scaled_blocks, scale):
    @pl.when(pl.program_id(1) == 0)
    def _():
        wb_ref[...] = w_ref[...].astype(BF16)

    if w_transposed:
        acc = lax.dot_general(a_ref[...], wb_ref[...], _NT, preferred_element_type=F32)
    else:
        acc = jnp.dot(a_ref[...], wb_ref[...], preferred_element_type=F32)
    if scaled_blocks:
        acc = acc * jnp.where(pl.program_id(0) < scaled_blocks, scale, 1.0)
    o_ref[...] = acc.astype(o_ref.dtype)


def _in_proj(a, w, n, w_transposed=False, scaled_cols=0, scale=1.0, bm=512, bn=1024):
    m, k = a.shape
    if w_transposed:
        w_spec = pl.BlockSpec((bn, k), lambda j, i: (j, 0))
    else:
        w_spec = pl.BlockSpec((k, bn), lambda j, i: (0, j))
    return pl.pallas_call(
        functools.partial(_mm_kernel, w_transposed=w_transposed, scaled_blocks=scaled_cols // bn, scale=scale),
        grid=(n // bn, m // bm),
        in_specs=[pl.BlockSpec((bm, k), lambda j, i: (i, 0)), w_spec],
        out_specs=pl.BlockSpec((bm, bn), lambda j, i: (i, j)),
        out_shape=jax.ShapeDtypeStruct((m, n), BF16),
        scratch_shapes=[pltpu.VMEM(w_spec.block_shape, BF16)],
        compiler_params=_params("arbitrary", "arbitrary"),
        name="in_proj",
    )(a, w)


def _mm_res_kernel(a_ref, b_ref, r_ref, o_ref):
    o_ref[...] = r_ref[...] + jnp.dot(a_ref[...], b_ref[...], preferred_element_type=F32)


def _out_proj(a, b, res, bm=512, bn=512):
    m, k = a.shape
    n = b.shape[1]
    return pl.pallas_call(
        _mm_res_kernel,
        grid=(n // bn, m // bm),
        in_specs=[pl.BlockSpec((bm, k), lambda j, i: (i, 0)),
                  pl.BlockSpec((k, bn), lambda j, i: (0, j)),
                  pl.BlockSpec((bm, bn), lambda j, i: (i, j))],
        out_specs=pl.BlockSpec((bm, bn), lambda j, i: (i, j)),
        out_shape=jax.ShapeDtypeStruct((m, n), F32),
        compiler_params=_params("arbitrary", "arbitrary"),
        name="out_proj",
    )(a, b, res)


def _silu(x):
    half = 0.5 * x
    return half + half * jnp.tanh(half)


def _hgrn_kernel(q_ref, f_ref, i_ref, g_ref, lbl_ref, on_ref, y_ref, st_ref, *, layer, tile, group):
    c = HGRN_CHUNK
    h2, h4, h8 = c // 2, c // 4, c // 8

    @pl.when(pl.program_id(2) == 0)
    def _():
        st_ref[...] = jnp.zeros_like(st_ref)

    lbl = lbl_ref[...]
    e = jnp.exp(lbl - jnp.max(lbl, axis=0, keepdims=True))
    lb_all = jnp.sum(e[:layer + 1], axis=0, keepdims=True) / jnp.sum(e, axis=0, keepdims=True)
    o_gain_all = on_ref[...]
    f_half = 0.5 * (1.0 - lb_all)
    f_mid = lb_all + f_half

    r = lax.broadcasted_iota(jnp.int32, (c, c), 0)
    s = lax.broadcasted_iota(jnp.int32, (c, c), 1)
    tri = (s <= r).astype(BF16)
    tri3 = jnp.concatenate([tri, tri, tri], axis=1)
    same_quarter = (r // h4) == (s // h4)
    level3_mask = jnp.logical_and(same_quarter, s <= r)

    width = group * HEAD_DIM
    heads = [slice(g * HEAD_DIM, (g + 1) * HEAD_DIM) for g in range(group)]

    def zeros(n):
        return jnp.zeros((n, width), BF16)

    def body(n, carry):
        rows = pl.ds(pl.multiple_of(n * c, c), c)
        q = q_ref[rows, :].astype(F32)
        f = f_ref[rows, :].astype(F32)
        v = i_ref[rows, :]
        gate = g_ref[rows, :].astype(F32)

        forget = f_mid + f_half * jnp.tanh(0.5 * f)
        k = 1.0 - forget
        log_f = jnp.log2(forget)
        qs = _silu(q)

        hi = log_f.astype(BF16)
        rem = log_f - hi.astype(F32)
        mid = rem.astype(BF16)
        lo = (rem - mid.astype(F32)).astype(BF16)
        gc = jnp.dot(tri3, jnp.concatenate([hi, mid, lo], axis=0), preferred_element_type=F32)

        ref1 = gc[h2 - 1:h2, :]
        q1 = jnp.concatenate([zeros(h2), (qs[h2:] * jnp.exp2(gc[h2:] - ref1)).astype(BF16)], axis=0)
        k1 = jnp.concatenate([(k[:h2] * jnp.exp2(ref1 - gc[:h2])).astype(BF16), zeros(h2)], axis=0)

        def quarter_pair(base):
            ref = gc[base + h4 - 1:base + h4, :]
            qq = (qs[base + h4:base + h2] * jnp.exp2(gc[base + h4:base + h2] - ref)).astype(BF16)
            kk = (k[base:base + h4] * jnp.exp2(ref - gc[base:base + h4])).astype(BF16)
            return qq, kk

        q2a, k2a = quarter_pair(0)
        q2b, k2b = quarter_pair(h2)
        q2a = jnp.concatenate([zeros(h4), q2a, zeros(h2)], axis=0)
        q2b = jnp.concatenate([zeros(h2 + h4), q2b], axis=0)
        k2a = jnp.concatenate([k2a, zeros(h2 + h4)], axis=0)
        k2b = jnp.concatenate([zeros(h2), k2b, zeros(h4)], axis=0)

        ref3 = jnp.concatenate(
            [jnp.broadcast_to(gc[m * h4 + h8 - 1:m * h4 + h8, :], (h4, width)) for m in range(4)], axis=0)
        d3 = gc - ref3
        q3 = (qs * jnp.exp2(d3)).astype(BF16)
        k3 = (k * jnp.exp2(-d3)).astype(BF16)

        g_last = gc[c - 1:c, :]
        q_dec = (qs * jnp.exp2(gc)).astype(BF16)
        k_end = (k * jnp.exp2(g_last - gc)).astype(BF16)
        decay = jnp.exp2(g_last)
        out_scale = o_gain_all * _silu(gate)

        def nt(a, b):
            return lax.dot_general(a, b, _NT, preferred_element_type=F32)

        s1 = [nt(q1[:, h], k1[:, h]) for h in heads]
        s2 = [nt(jnp.concatenate([q2a[:, h], q2b[:, h]], axis=1),
                 jnp.concatenate([k2a[:, h], k2b[:, h]], axis=1)) for h in heads]
        s3 = [nt(q3[:, h], k3[:, h]) for h in heads]
        scores = [(s1[g] + s2[g] + jnp.where(level3_mask, s3[g], 0.0)).astype(BF16) for g in range(group)]

        st = [st_ref[g] for g in range(group)]
        o = [jnp.dot(scores[g], v[:, h], preferred_element_type=F32) + nt(q_dec[:, h], st[g].astype(BF16))
             for g, h in enumerate(heads)]
        kv = [lax.dot_general(v[:, h], k_end[:, h], _TN, preferred_element_type=F32) for h in heads]
        for g, h in enumerate(heads):
            st_ref[g] = st[g] * decay[:, h] + kv[g]
            og = o[g] * lax.rsqrt(jnp.mean(o[g] * o[g], axis=-1, keepdims=True) + EPS)
            y_ref[rows, h] = (og * out_scale[:, h]).astype(y_ref.dtype)
        return carry

    lax.fori_loop(0, tile // c, body, 0)


def _hgrn_mixer(proj, lb_logits, o_norm, *, layer, batch, seq, tile=512, group=8):
    m, four_inner = proj.shape
    d_inner = four_inner // 4
    heads = d_inner // HEAD_DIM
    ng = heads // group
    nt = seq // tile
    width = group * HEAD_DIM

    def stream(idx):
        return pl.BlockSpec((tile, width), lambda b, h, t: (b * nt + t, idx * ng + h))

    rows = lb_logits.shape[0]
    return pl.pallas_call(
        functools.partial(_hgrn_kernel, layer=layer, tile=tile, group=group),
        grid=(batch, ng, nt),
        in_specs=[stream(0), stream(1), stream(2), stream(3),
                  pl.BlockSpec((rows, width), lambda b, h, t: (0, h)),
                  pl.BlockSpec((1, width), lambda b, h, t: (0, h))],
        out_specs=pl.BlockSpec((tile, width), lambda b, h, t: (b * nt + t, h)),
        out_shape=jax.ShapeDtypeStruct((m, d_inner), BF16),
        scratch_shapes=[pltpu.VMEM((group, HEAD_DIM, HEAD_DIM), F32)],
        compiler_params=_params("arbitrary", "arbitrary", "arbitrary"),
        name="hgrn2_mixer",
    )(proj, proj, proj, proj, lb_logits, o_norm.reshape(1, d_inner))


def _fgate_kernel(wt_ref, h_ref, b_ref, c_ref, carry_ref, *, bs):
    @pl.when(pl.program_id(1) == 0)
    def _():
        carry_ref[...] = jnp.zeros_like(carry_ref)

    z = lax.dot_general(wt_ref[...].astype(BF16), h_ref[...], _NT, preferred_element_type=F32) + b_ref[...]
    ls = jnp.minimum(z, 0.0) - jnp.log1p(jnp.exp(-jnp.abs(z)))
    r = lax.broadcasted_iota(jnp.int32, (bs, bs), 0)
    s = lax.broadcasted_iota(jnp.int32, (bs, bs), 1)
    upper = (r <= s).astype(F32)
    cs = jnp.dot(ls, upper, precision=lax.Precision.HIGHEST, preferred_element_type=F32) + carry_ref[...]
    c_ref[0] = cs * LOG2E
    carry_ref[...] = cs[:, bs - 1:bs]


def _forget_cumsum(h, w_f_t, b_f, *, batch, seq, bs=512):
    heads, d = w_f_t.shape
    nb = seq // bs
    return pl.pallas_call(
        functools.partial(_fgate_kernel, bs=bs),
        grid=(batch, nb),
        in_specs=[pl.BlockSpec((heads, d), lambda b, t: (0, 0)),
                  pl.BlockSpec((bs, d), lambda b, t: (b * nb + t, 0)),
                  pl.BlockSpec((heads, 1), lambda b, t: (0, 0))],
        out_specs=pl.BlockSpec((1, heads, bs), lambda b, t: (b, 0, t)),
        out_shape=jax.ShapeDtypeStruct((batch, heads, seq), F32),
        scratch_shapes=[pltpu.VMEM((heads, 1), F32)],
        compiler_params=_params("arbitrary", "arbitrary"),
        name="forget_cumsum",
    )(w_f_t, h, b_f.reshape(heads, 1))


def _fox_kernel(q_ref, k_ref, v_ref, g_ref, c_ref, y_ref, sa_ref, sb_ref, m_ref, l_ref, acc_ref, *, blk, group):
    i = pl.program_id(2)
    m_ref[...] = jnp.full_like(m_ref, -jnp.inf)
    l_ref[...] = jnp.zeros_like(l_ref)
    acc_ref[...] = jnp.zeros_like(acc_ref)
    nlane = blk // HEAD_DIM

    def head_cols(g):
        return slice(g * HEAD_DIM, (g + 1) * HEAD_DIM)

    def key_rows(j):
        return pl.ds(pl.multiple_of(j * blk, blk), blk)

    def logits(j, s_ref):
        for g in range(group):
            qk = lax.dot_general(q_ref[:, head_cols(g)], k_ref[key_rows(j), head_cols(g)], _NT,
                                 preferred_element_type=F32)
            s_ref[g] = qk - c_ref[g, pl.ds(j, 1), :]

    def softmax_pv(j, s_ref, masked):
        for g in range(group):
            sc = s_ref[g]
            if masked:
                r = lax.broadcasted_iota(jnp.int32, (blk, blk), 0)
                s = lax.broadcasted_iota(jnp.int32, (blk, blk), 1)
                sc = jnp.where(s <= r, sc, -jnp.inf)
            m_prev = m_ref[g]
            m_new = jnp.maximum(m_prev, jnp.max(sc, axis=-1, keepdims=True))
            alpha = jnp.exp2(m_prev - m_new)
            p = jnp.exp2(sc - jnp.concatenate([m_new] * nlane, axis=-1))
            p_part = p[:, :HEAD_DIM]
            for t in range(1, nlane):
                p_part = p_part + p[:, t * HEAD_DIM:(t + 1) * HEAD_DIM]
            l_ref[g] = alpha * l_ref[g] + p_part
            acc_ref[g] = alpha * acc_ref[g] + jnp.dot(p.astype(BF16), v_ref[key_rows(j), head_cols(g)],
                                                      preferred_element_type=F32)
            m_ref[g] = m_new

    logits(0, sa_ref)

    def body(jj, carry):
        logits(2 * jj + 1, sb_ref)
        softmax_pv(2 * jj, sa_ref, False)
        logits(2 * jj + 2, sa_ref)
        softmax_pv(2 * jj + 1, sb_ref, False)
        return carry

    lax.fori_loop(0, i // 2, body, 0)

    @pl.when(i % 2 == 0)
    def _():
        softmax_pv(i, sa_ref, True)

    @pl.when(i % 2 == 1)
    def _():
        logits(i, sb_ref)
        softmax_pv(i - 1, sa_ref, False)
        softmax_pv(i, sb_ref, True)

    for g in range(group):
        cols = slice(g * HEAD_DIM, (g + 1) * HEAD_DIM)
        o = acc_ref[g] / jnp.sum(l_ref[g], axis=-1, keepdims=True)
        y_ref[:, cols] = (o * _silu(g_ref[:, cols].astype(F32))).astype(y_ref.dtype)


def _fox_attention(proj, c, *, batch, seq, blk=512, group=4):
    m = proj.shape[0]
    d_inner = proj.shape[1] // 4
    heads = d_inner // HEAD_DIM
    ng = heads // group
    nq = seq // blk
    width = group * HEAD_DIM
    c_blocks = c.reshape(batch * heads, nq, blk)

    return pl.pallas_call(
        functools.partial(_fox_kernel, blk=blk, group=group),
        grid=(batch, ng, nq),
        in_specs=[pl.BlockSpec((blk, width), lambda b, h, i: (b * nq + i, h)),
                  pl.BlockSpec((seq, width), lambda b, h, i: (b, ng + h)),
                  pl.BlockSpec((seq, width), lambda b, h, i: (b, 2 * ng + h)),
                  pl.BlockSpec((blk, width), lambda b, h, i: (b * nq + i, 3 * ng + h)),
                  pl.BlockSpec((group, nq, blk), lambda b, h, i: (b * ng + h, 0, 0))],
        out_specs=pl.BlockSpec((blk, width), lambda b, h, i: (b * nq + i, h)),
        out_shape=jax.ShapeDtypeStruct((m, d_inner), BF16),
        scratch_shapes=[pltpu.VMEM((group, blk, blk), F32),
                        pltpu.VMEM((group, blk, blk), F32),
                        pltpu.VMEM((group, blk, HEAD_DIM), F32),
                        pltpu.VMEM((group, blk, HEAD_DIM), F32),
                        pltpu.VMEM((group, blk, HEAD_DIM), F32)],
        compiler_params=_params("arbitrary", "arbitrary", "arbitrary"),
        name="fox_attention",
    )(proj, proj, proj, proj, c_blocks)


def kernel(x, norm_w, w_in_a, lb_logits, o_norm_a, w_out_a, w_in_b, b_f, w_out_b, final_norm):
    batch, seq, d_model = x.shape
    depth = norm_w.shape[0]
    xr = x.reshape(batch * seq, d_model)
    for layer in range(depth):
        h = _rmsnorm(xr, norm_w[layer], BF16)
        j = layer // 2
        if layer % 2 == 0:
            d_inner = w_out_a.shape[1]
            proj = _in_proj(h, w_in_a[j], 4 * d_inner)
            y = _hgrn_mixer(proj, lb_logits, o_norm_a[j], layer=j, batch=batch, seq=seq)
            xr = _out_proj(y, w_out_a[j].astype(BF16), xr)
        else:
            d_inner = w_out_b.shape[1]
            w_b_t = w_in_b[j].T
            proj = _in_proj(h, w_b_t, 4 * d_inner, w_transposed=True,
                            scaled_cols=d_inner, scale=LOG2E / math.sqrt(HEAD_DIM))
            c = _forget_cumsum(h, w_b_t[4 * d_inner:], b_f[j], batch=batch, seq=seq)
            y = _fox_attention(proj, c, batch=batch, seq=seq)
            xr = _out_proj(y, w_out_b[j].astype(BF16), xr)
    return _rmsnorm(xr, final_norm, x.dtype).reshape(batch, seq, d_model)
```

```python
import functools
import math

import jax
import jax.numpy as jnp
from jax import lax
from jax.experimental import pallas as pl
from jax.experimental.pallas import tpu as pltpu

F32 = jnp.float32
BF16 = jnp.bfloat16

EPS = 1e-6
HEAD_DIM = 128
HGRN_BLOCK = 16
HGRN_CHUNK = 8 * HGRN_BLOCK
VMEM_LIMIT = 56 * 1024 * 1024
LOG2E = math.log2(math.e)

_NT = (((1,), (1,)), ((), ()))
_TN = (((0,), (0,)), ((), ()))


def _params(*sem):
    return pltpu.CompilerParams(dimension_semantics=sem, vmem_limit_bytes=VMEM_LIMIT)


def _rmsnorm_kernel(x_ref, w_ref, o_ref):
    x = x_ref[...]
    y = x * lax.rsqrt(jnp.mean(x * x, axis=-1, keepdims=True) + EPS)
    o_ref[...] = (y * w_ref[...]).astype(o_ref.dtype)


def _rmsnorm(x, w, out_dtype, bm=256):
    m, d = x.shape
    return pl.pallas_call(
        _rmsnorm_kernel,
        grid=(m // bm,),
        in_specs=[pl.BlockSpec((bm, d), lambda i: (i, 0)),
                  pl.BlockSpec((1, d), lambda i: (0, 0))],
        out_specs=pl.BlockSpec((bm, d), lambda i: (i, 0)),
        out_shape=jax.ShapeDtypeStruct((m, d), out_dtype),
        compiler_params=_params("arbitrary"),
        name="rmsnorm",
    )(x, w.reshape(1, d))


def _mm_kernel(a_ref, w_ref, o_ref, *wb_ref, w_transposed, scaled_blocks, scale):
    if wb_ref:
        @pl.when(pl.program_id(1) == 0)
        def _():
            wb_ref[0][...] = w_ref[...].astype(BF16)
        w = wb_ref[0][...]
    else:
        w = w_ref[...]
    if w_transposed:
        acc = lax.dot_general(a_ref[...], w, _NT, preferred_element_type=F32)
    else:
        acc = jnp.dot(a_ref[...], w, preferred_element_type=F32)
    if scaled_blocks:
        acc = acc * jnp.where(pl.program_id(0) < scaled_blocks, scale, 1.0)
    o_ref[...] = acc.astype(o_ref.dtype)


def _in_proj(a, w, n, w_transposed=False, scaled_cols=0, scale=1.0, bn=1024):
    m, k = a.shape
    in_kernel_cast = w.dtype != BF16
    bm = 512 if in_kernel_cast else 1024
    if w_transposed:
        w_spec = pl.BlockSpec((bn, k), lambda j, i: (j, 0))
    else:
        w_spec = pl.BlockSpec((k, bn), lambda j, i: (0, j))
    return pl.pallas_call(
        functools.partial(_mm_kernel, w_transposed=w_transposed, scaled_blocks=scaled_cols // bn, scale=scale),
        grid=(n // bn, m // bm),
        in_specs=[pl.BlockSpec((bm, k), lambda j, i: (i, 0)), w_spec],
        out_specs=pl.BlockSpec((bm, bn), lambda j, i: (i, j)),
        out_shape=jax.ShapeDtypeStruct((m, n), BF16),
        scratch_shapes=[pltpu.VMEM(w_spec.block_shape, BF16)] if in_kernel_cast else [],
        compiler_params=_params("arbitrary", "arbitrary"),
        name="in_proj",
    )(a, w)


def _mm_res_kernel(a_ref, b_ref, r_ref, o_ref):
    o_ref[...] = r_ref[...] + jnp.dot(a_ref[...], b_ref[...], preferred_element_type=F32)


def _out_proj(a, b, res, bm=512, bn=512):
    m, k = a.shape
    n = b.shape[1]
    return pl.pallas_call(
        _mm_res_kernel,
        grid=(n // bn, m // bm),
        in_specs=[pl.BlockSpec((bm, k), lambda j, i: (i, 0)),
                  pl.BlockSpec((k, bn), lambda j, i: (0, j)),
                  pl.BlockSpec((bm, bn), lambda j, i: (i, j))],
        out_specs=pl.BlockSpec((bm, bn), lambda j, i: (i, j)),
        out_shape=jax.ShapeDtypeStruct((m, n), F32),
        compiler_params=_params("arbitrary", "arbitrary"),
        name="out_proj",
    )(a, b, res)


def _rounding_jobs(jobs, grid):
    steps = math.prod(grid)

    def slab(*idx):
        lin = 0
        for i, g in zip(idx, grid):
            lin = lin * g + i
        return (lin, 0)

    specs, shapes = [], []
    for w, rows in jobs:
        specs.append(pl.BlockSpec((rows // steps, w.shape[1]), slab))
        shapes.append(jax.ShapeDtypeStruct((rows, w.shape[1]), BF16))
    return specs, shapes


def _round_slabs(src_refs, dst_refs):
    for src, dst in zip(src_refs, dst_refs):
        dst[...] = src[...].astype(dst.dtype)


def _silu(x):
    half = 0.5 * x
    return half + half * jnp.tanh(half)


def _hgrn_kernel(q_ref, f_ref, i_ref, g_ref, lbl_ref, on_ref, *refs, layer, tile, group, njobs):
    y_ref, st_ref = refs[njobs], refs[2 * njobs + 1]
    _round_slabs(refs[:njobs], refs[njobs + 1:2 * njobs + 1])

    c = HGRN_CHUNK
    h2, h4, h8 = c // 2, c // 4, c // 8

    @pl.when(pl.program_id(2) == 0)
    def _():
        st_ref[...] = jnp.zeros_like(st_ref)

    lbl = lbl_ref[...]
    e = jnp.exp(lbl - jnp.max(lbl, axis=0, keepdims=True))
    lb_all = jnp.sum(e[:layer + 1], axis=0, keepdims=True) / jnp.sum(e, axis=0, keepdims=True)
    o_gain_all = on_ref[...]
    f_half = 0.5 * (1.0 - lb_all)
    f_mid = lb_all + f_half

    r = lax.broadcasted_iota(jnp.int32, (c, c), 0)
    s = lax.broadcasted_iota(jnp.int32, (c, c), 1)
    tri = (s <= r).astype(BF16)
    tri3 = jnp.concatenate([tri, tri, tri], axis=1)
    same_quarter = (r // h4) == (s // h4)
    level3_mask = jnp.logical_and(same_quarter, s <= r)

    width = group * HEAD_DIM
    heads = [slice(g * HEAD_DIM, (g + 1) * HEAD_DIM) for g in range(group)]

    def zeros(n):
        return jnp.zeros((n, width), BF16)

    def body(n, carry):
        rows = pl.ds(pl.multiple_of(n * c, c), c)
        q = q_ref[rows, :].astype(F32)
        f = f_ref[rows, :].astype(F32)
        v = i_ref[rows, :]
        gate = g_ref[rows, :].astype(F32)

        forget = f_mid + f_half * jnp.tanh(0.5 * f)
        k = 1.0 - forget
        log_f = jnp.log2(forget)
        qs = _silu(q)

        hi = log_f.astype(BF16)
        rem = log_f - hi.astype(F32)
        mid = rem.astype(BF16)
        lo = (rem - mid.astype(F32)).astype(BF16)
        gc = jnp.dot(tri3, jnp.concatenate([hi, mid, lo], axis=0), preferred_element_type=F32)

        ref1 = gc[h2 - 1:h2, :]
        q1 = jnp.concatenate([zeros(h2), (qs[h2:] * jnp.exp2(gc[h2:] - ref1)).astype(BF16)], axis=0)
        k1 = jnp.concatenate([(k[:h2] * jnp.exp2(ref1 - gc[:h2])).astype(BF16), zeros(h2)], axis=0)

        def quarter_pair(base):
            ref = gc[base + h4 - 1:base + h4, :]
            qq = (qs[base + h4:base + h2] * jnp.exp2(gc[base + h4:base + h2] - ref)).astype(BF16)
            kk = (k[base:base + h4] * jnp.exp2(ref - gc[base:base + h4])).astype(BF16)
            return qq, kk

        q2a, k2a = quarter_pair(0)
        q2b, k2b = quarter_pair(h2)
        q2a = jnp.concatenate([zeros(h4), q2a, zeros(h2)], axis=0)
        q2b = jnp.concatenate([zeros(h2 + h4), q2b], axis=0)
        k2a = jnp.concatenate([k2a, zeros(h2 + h4)], axis=0)
        k2b = jnp.concatenate([zeros(h2), k2b, zeros(h4)], axis=0)

        ref3 = jnp.concatenate(
            [jnp.broadcast_to(gc[m * h4 + h8 - 1:m * h4 + h8, :], (h4, width)) for m in range(4)], axis=0)
        d3 = gc - ref3
        q3 = (qs * jnp.exp2(d3)).astype(BF16)
        k3 = (k * jnp.exp2(-d3)).astype(BF16)

        g_last = gc[c - 1:c, :]
        q_dec = (qs * jnp.exp2(gc)).astype(BF16)
        k_end = (k * jnp.exp2(g_last - gc)).astype(BF16)
        decay = jnp.exp2(g_last)
        out_scale = o_gain_all * _silu(gate)

        def nt(a, b):
            return lax.dot_general(a, b, _NT, preferred_element_type=F32)

        s1 = [nt(q1[:, h], k1[:, h]) for h in heads]
        s2 = [nt(jnp.concatenate([q2a[:, h], q2b[:, h]], axis=1),
                 jnp.concatenate([k2a[:, h], k2b[:, h]], axis=1)) for h in heads]
        s3 = [nt(q3[:, h], k3[:, h]) for h in heads]
        scores = [(s1[g] + s2[g] + jnp.where(level3_mask, s3[g], 0.0)).astype(BF16) for g in range(group)]

        st = [st_ref[g] for g in range(group)]
        o = [jnp.dot(scores[g], v[:, h], preferred_element_type=F32) + nt(q_dec[:, h], st[g].astype(BF16))
             for g, h in enumerate(heads)]
        kv = [lax.dot_general(v[:, h], k_end[:, h], _TN, preferred_element_type=F32) for h in heads]
        for g, h in enumerate(heads):
            st_ref[g] = st[g] * decay[:, h] + kv[g]
            og = o[g] * lax.rsqrt(jnp.mean(o[g] * o[g], axis=-1, keepdims=True) + EPS)
            y_ref[rows, h] = (og * out_scale[:, h]).astype(y_ref.dtype)
        return carry

    lax.fori_loop(0, tile // c, body, 0)


def _hgrn_mixer(proj, lb_logits, o_norm, jobs, *, layer, batch, seq, tile=512, group=8):
    m, four_inner = proj.shape
    d_inner = four_inner // 4
    heads = d_inner // HEAD_DIM
    ng = heads // group
    nt = seq // tile
    width = group * HEAD_DIM

    def stream(idx):
        return pl.BlockSpec((tile, width), lambda b, h, t: (b * nt + t, idx * ng + h))

    rows = lb_logits.shape[0]
    grid = (batch, ng, nt)
    job_specs, job_shapes = _rounding_jobs(jobs, grid)
    y, *rounded = pl.pallas_call(
        functools.partial(_hgrn_kernel, layer=layer, tile=tile, group=group, njobs=len(jobs)),
        grid=grid,
        in_specs=[stream(0), stream(1), stream(2), stream(3),
                  pl.BlockSpec((rows, width), lambda b, h, t: (0, h)),
                  pl.BlockSpec((1, width), lambda b, h, t: (0, h))] + job_specs,
        out_specs=[pl.BlockSpec((tile, width), lambda b, h, t: (b * nt + t, h))] + job_specs,
        out_shape=[jax.ShapeDtypeStruct((m, d_inner), BF16)] + job_shapes,
        scratch_shapes=[pltpu.VMEM((group, HEAD_DIM, HEAD_DIM), F32)],
        compiler_params=_params("arbitrary", "arbitrary", "arbitrary"),
        name="hgrn2_mixer",
    )(proj, proj, proj, proj, lb_logits, o_norm.reshape(1, d_inner), *[w for w, _ in jobs])
    return y, rounded


def _fgate_kernel(wt_ref, h_ref, b_ref, c_ref, carry_ref, *, bs):
    @pl.when(pl.program_id(1) == 0)
    def _():
        carry_ref[...] = jnp.zeros_like(carry_ref)

    z = lax.dot_general(wt_ref[...].astype(BF16), h_ref[...], _NT, preferred_element_type=F32) + b_ref[...]
    ls = jnp.minimum(z, 0.0) - jnp.log1p(jnp.exp(-jnp.abs(z)))
    r = lax.broadcasted_iota(jnp.int32, (bs, bs), 0)
    s = lax.broadcasted_iota(jnp.int32, (bs, bs), 1)
    upper = (r <= s).astype(F32)
    cs = jnp.dot(ls, upper, precision=lax.Precision.HIGHEST, preferred_element_type=F32) + carry_ref[...]
    c_ref[0] = cs * LOG2E
    carry_ref[...] = cs[:, bs - 1:bs]


def _forget_cumsum(h, w_f_t, b_f, *, batch, seq, bs=512):
    heads, d = w_f_t.shape
    nb = seq // bs
    return pl.pallas_call(
        functools.partial(_fgate_kernel, bs=bs),
        grid=(batch, nb),
        in_specs=[pl.BlockSpec((heads, d), lambda b, t: (0, 0)),
                  pl.BlockSpec((bs, d), lambda b, t: (b * nb + t, 0)),
                  pl.BlockSpec((heads, 1), lambda b, t: (0, 0))],
        out_specs=pl.BlockSpec((1, heads, bs), lambda b, t: (b, 0, t)),
        out_shape=jax.ShapeDtypeStruct((batch, heads, seq), F32),
        scratch_shapes=[pltpu.VMEM((heads, 1), F32)],
        compiler_params=_params("arbitrary", "arbitrary"),
        name="forget_cumsum",
    )(w_f_t, h, b_f.reshape(heads, 1))


def _fox_kernel(q_ref, k_ref, v_ref, g_ref, c_ref, *refs, blk, group, njobs):
    y_ref = refs[njobs]
    sa_ref, sb_ref, m_ref, l_ref, acc_ref = refs[2 * njobs + 1:]
    _round_slabs(refs[:njobs], refs[njobs + 1:2 * njobs + 1])

    i = pl.program_id(2)
    m_ref[...] = jnp.full_like(m_ref, -jnp.inf)
    l_ref[...] = jnp.zeros_like(l_ref)
    acc_ref[...] = jnp.zeros_like(acc_ref)
    nlane = blk // HEAD_DIM

    def head_cols(g):
        return slice(g * HEAD_DIM, (g + 1) * HEAD_DIM)

    def key_rows(j):
        return pl.ds(pl.multiple_of(j * blk, blk), blk)

    def logits(j, s_ref):
        for g in range(group):
            qk = lax.dot_general(q_ref[:, head_cols(g)], k_ref[key_rows(j), head_cols(g)], _NT,
                                 preferred_element_type=F32)
            s_ref[g] = qk - c_ref[g, pl.ds(j, 1), :]

    def softmax_pv(j, s_ref, masked):
        for g in range(group):
            sc = s_ref[g]
            if masked:
                r = lax.broadcasted_iota(jnp.int32, (blk, blk), 0)
                s = lax.broadcasted_iota(jnp.int32, (blk, blk), 1)
                sc = jnp.where(s <= r, sc, -jnp.inf)
            m_prev = m_ref[g]
            m_new = jnp.maximum(m_prev, jnp.max(sc, axis=-1, keepdims=True))
            alpha = jnp.exp2(m_prev - m_new)
            p = jnp.exp2(sc - jnp.concatenate([m_new] * nlane, axis=-1))
            p_part = p[:, :HEAD_DIM]
            for t in range(1, nlane):
                p_part = p_part + p[:, t * HEAD_DIM:(t + 1) * HEAD_DIM]
            l_ref[g] = alpha * l_ref[g] + p_part
            acc_ref[g] = alpha * acc_ref[g] + jnp.dot(p.astype(BF16), v_ref[key_rows(j), head_cols(g)],
                                                      preferred_element_type=F32)
            m_ref[g] = m_new

    logits(0, sa_ref)

    def body(jj, carry):
        logits(2 * jj + 1, sb_ref)
        softmax_pv(2 * jj, sa_ref, False)
        logits(2 * jj + 2, sa_ref)
        softmax_pv(2 * jj + 1, sb_ref, False)
        return carry

    lax.fori_loop(0, i // 2, body, 0)

    @pl.when(i % 2 == 0)
    def _():
        softmax_pv(i, sa_ref, True)

    @pl.when(i % 2 == 1)
    def _():
        logits(i, sb_ref)
        softmax_pv(i - 1, sa_ref, False)
        softmax_pv(i, sb_ref, True)

    for g in range(group):
        cols = slice(g * HEAD_DIM, (g + 1) * HEAD_DIM)
        o = acc_ref[g] / jnp.sum(l_ref[g], axis=-1, keepdims=True)
        y_ref[:, cols] = (o * _silu(g_ref[:, cols].astype(F32))).astype(y_ref.dtype)


def _fox_attention(proj, c, jobs, *, batch, seq, blk=512, group=4):
    m = proj.shape[0]
    d_inner = proj.shape[1] // 4
    heads = d_inner // HEAD_DIM
    ng = heads // group
    nq = seq // blk
    width = group * HEAD_DIM
    c_blocks = c.reshape(batch * heads, nq, blk)

    grid = (batch, ng, nq)
    job_specs, job_shapes = _rounding_jobs(jobs, grid)
    y, *rounded = pl.pallas_call(
        functools.partial(_fox_kernel, blk=blk, group=group, njobs=len(jobs)),
        grid=grid,
        in_specs=[pl.BlockSpec((blk, width), lambda b, h, i: (b * nq + i, h)),
                  pl.BlockSpec((seq, width), lambda b, h, i: (b, ng + h)),
                  pl.BlockSpec((seq, width), lambda b, h, i: (b, 2 * ng + h)),
                  pl.BlockSpec((blk, width), lambda b, h, i: (b * nq + i, 3 * ng + h)),
                  pl.BlockSpec((group, nq, blk), lambda b, h, i: (b * ng + h, 0, 0))] + job_specs,
        out_specs=[pl.BlockSpec((blk, width), lambda b, h, i: (b * nq + i, h))] + job_specs,
        out_shape=[jax.ShapeDtypeStruct((m, d_inner), BF16)] + job_shapes,
        scratch_shapes=[pltpu.VMEM((group, blk, blk), F32),
                        pltpu.VMEM((group, blk, blk), F32),
                        pltpu.VMEM((group, blk, HEAD_DIM), F32),
                        pltpu.VMEM((group, blk, HEAD_DIM), F32),
                        pltpu.VMEM((group, blk, HEAD_DIM), F32)],
        compiler_params=_params("arbitrary", "arbitrary", "arbitrary"),
        name="fox_attention",
    )(proj, proj, proj, proj, c_blocks, *[w for w, _ in jobs])
    return y, rounded


def kernel(x, norm_w, w_in_a, lb_logits, o_norm_a, w_out_a, w_in_b, b_f, w_out_b, final_norm):
    batch, seq, d_model = x.shape
    depth = norm_w.shape[0]
    d_inner = w_out_a.shape[1]
    xr = x.reshape(batch * seq, d_model)
    w_in_rounded = None
    for layer in range(depth):
        h = _rmsnorm(xr, norm_w[layer], BF16)
        j = layer // 2
        last = layer + 1 == depth
        if layer % 2 == 0:
            w_in = w_in_a[j] if w_in_rounded is None else w_in_rounded
            proj = _in_proj(h, w_in, 4 * d_inner)
            jobs = [(w_out_a[j], d_inner)] + ([] if last else [(w_in_b[j].T, 4 * d_inner)])
            y, rounded = _hgrn_mixer(proj, lb_logits, o_norm_a[j], jobs, layer=j, batch=batch, seq=seq)
        else:
            w_b_t = w_in_b[j].T
            w_in = w_b_t if w_in_rounded is None else w_in_rounded
            proj = _in_proj(h, w_in, 4 * d_inner, w_transposed=True,
                            scaled_cols=d_inner, scale=LOG2E / math.sqrt(HEAD_DIM))
            c = _forget_cumsum(h, w_b_t[4 * d_inner:], b_f[j], batch=batch, seq=seq)
            jobs = [(w_out_b[j], d_inner)] + ([] if last else [(w_in_a[j + 1], d_model)])
            y, rounded = _fox_attention(proj, c, jobs, batch=batch, seq=seq)
        xr = _out_proj(y, rounded[0], xr)
        w_in_rounded = rounded[1] if len(rounded) > 1 else None
    return _rmsnorm(xr, final_norm, x.dtype).reshape(batch, seq, d_model)
```
